```python
import math
import jax, jax.numpy as jnp
from jax import lax
import numpy as np

D_MODEL = 2048
BATCH = 8
SEQ = 2048
DEPTH = 2

CHUNK = 64
QBLOCK = 128
HEAD_DIM = 128
SB_HEADS = 8
DIFF_HEADS = 4
DIFF_V_DIM = 2 * HEAD_DIM
ROT_DIM = HEAD_DIM // 4
ROPE_THETA = 500000.0
SG_BLOCK = 128
SG_GROUPS = 16
SG_GROUP_DIM = 128
SG_WIDTH = SG_GROUPS * SG_GROUP_DIM
D_FF = 4 * D_MODEL
PLE_DIM = 256
N_EVEN = (DEPTH + 1) // 2
N_ODD = DEPTH // 2
SB_WIDTH = SB_HEADS * HEAD_DIM
DIFF_QK_WIDTH = DIFF_HEADS * 2 * HEAD_DIM
DIFF_V_WIDTH = DIFF_HEADS * DIFF_V_DIM
EVEN_IN_WIDTH = 3 * SB_WIDTH + 2 * DIFF_QK_WIDTH + DIFF_V_WIDTH
EVEN_OUT_WIDTH = SB_WIDTH + DIFF_V_WIDTH
EVEN_SPLITS = [SB_WIDTH, 2 * SB_WIDTH, 3 * SB_WIDTH,
               3 * SB_WIDTH + DIFF_QK_WIDTH, 3 * SB_WIDTH + 2 * DIFF_QK_WIDTH]
EPS = 1e-6

kernel_name = 'hybrid_sb_diff_gmlp_stream_block'


def rms_norm(x, g):
    x32 = x.astype(jnp.float32)
    y = x32 * lax.rsqrt(jnp.mean(x32 * x32, axis=-1, keepdims=True) + EPS)
    return (y * g.astype(jnp.float32)).astype(x.dtype)


def layer_norm(x, g, b):
    x32 = x.astype(jnp.float32)
    mu = jnp.mean(x32, axis=-1, keepdims=True)
    xc = x32 - mu
    y = xc * lax.rsqrt(jnp.mean(xc * xc, axis=-1, keepdims=True) + EPS)
    return (y * g.astype(jnp.float32) + b.astype(jnp.float32)).astype(x.dtype)


def partial_rope(x, cos, sin):
    half = ROT_DIM // 2
    x1 = x[..., :half].astype(jnp.float32)
    x2 = x[..., half:ROT_DIM].astype(jnp.float32)
    rot = jnp.concatenate([x1 * cos - x2 * sin, x2 * cos + x1 * sin], axis=-1).astype(x.dtype)
    return jnp.concatenate([rot, x[..., ROT_DIM:]], axis=-1)


def stick_breaking_attention(q, k, v):
    S = q.shape[1]
    scale = HEAD_DIM ** -0.5
    outs = []
    for qs in range(0, S, QBLOCK):
        kend = qs + QBLOCK
        qb = q[:, qs:kend].astype(jnp.float32)
        kb = k[:, :kend].astype(jnp.float32)
        z = jnp.einsum('bthd,bshd->bhts', qb, kb) * scale
        t_idx = qs + jnp.arange(QBLOCK)[:, None]
        s_idx = jnp.arange(kend)[None, :]
        mask = s_idx < t_idx
        log_keep = jnp.where(mask, jax.nn.log_sigmoid(-z), 0.0)
        later = lax.cumsum(log_keep, axis=3, reverse=True) - log_keep
        w = jnp.where(mask, jnp.exp(jax.nn.log_sigmoid(z) + later), 0.0)
        outs.append(jnp.einsum('bhts,bshd->bthd', w.astype(v.dtype), v[:, :kend]))
    return jnp.concatenate(outs, axis=1)


def differential_attention(q, k, v, lam, lambda_init, subln_g):
    S = q.shape[1]
    scale = HEAD_DIM ** -0.5
    outs = []
    for qs in range(0, S, QBLOCK):
        kend = qs + QBLOCK
        qb = q[:, qs:kend].astype(jnp.float32)
        kb = k[:, :kend].astype(jnp.float32)
        z = jnp.einsum('bthmd,bshmd->bhmts', qb, kb) * scale
        t_chunk = (qs + jnp.arange(QBLOCK)[:, None]) // CHUNK
        s_chunk = jnp.arange(kend)[None, :] // CHUNK
        z = jnp.where(s_chunk <= t_chunk, z, -jnp.inf)
        a = jax.nn.softmax(z, axis=-1)
        w = a[:, :, 0] - lam * a[:, :, 1]
        outs.append(jnp.einsum('bhts,bshe->bthe', w.astype(v.dtype), v[:, :kend]))
    o = jnp.concatenate(outs, axis=1)
    return rms_norm(o, subln_g) * (1.0 - lambda_init)


def even_mixer(h, w_in, lam_q1, lam_k1, lam_q2, lam_k2, subln_g, w_out, cos, sin, lambda_init):
    B, S, _ = h.shape
    proj = h @ w_in
    sb_q, sb_k, sb_v, df_q, df_k, df_v = jnp.split(proj, EVEN_SPLITS, axis=-1)
    sb_shape = (B, S, SB_HEADS, HEAD_DIM)
    sb_o = stick_breaking_attention(sb_q.reshape(sb_shape), sb_k.reshape(sb_shape),
                                    sb_v.reshape(sb_shape))
    qk_shape = (B, S, DIFF_HEADS * 2, HEAD_DIM)
    df_q = partial_rope(df_q.reshape(qk_shape), cos, sin).reshape(B, S, DIFF_HEADS, 2, HEAD_DIM)
    df_k = partial_rope(df_k.reshape(qk_shape), cos, sin).reshape(B, S, DIFF_HEADS, 2, HEAD_DIM)
    df_v = df_v.reshape(B, S, DIFF_HEADS, DIFF_V_DIM)
    lam = (jnp.exp(jnp.sum(lam_q1.astype(jnp.float32) * lam_k1.astype(jnp.float32)))
           - jnp.exp(jnp.sum(lam_q2.astype(jnp.float32) * lam_k2.astype(jnp.float32)))
           + lambda_init)
    df_o = differential_attention(df_q, df_k, df_v, lam, lambda_init, subln_g)
    merged = jnp.concatenate([sb_o.reshape(B, S, SB_WIDTH), df_o.reshape(B, S, DIFF_V_WIDTH)], axis=-1)
    return merged @ w_out


def odd_mixer(h, w_in, ln_g, ln_b, w_s, b_s, w_out):
    B, S, _ = h.shape
    u, v = jnp.split(jax.nn.gelu(h @ w_in), 2, axis=-1)
    v = layer_norm(v, ln_g, ln_b)
    n_blk = S // SG_BLOCK
    v = v.reshape(B, n_blk, SG_BLOCK, SG_GROUPS, SG_GROUP_DIM)
    pos = jnp.arange(SG_BLOCK)
    mask = (pos[None, :] // CHUNK) <= (pos[:, None] // CHUNK)
    w = jnp.where(mask[None], w_s, 0.0)
    mixed = jnp.einsum('gts,bnsgc->bntgc', w, v) + b_s.T[None, None, :, :, None]
    y = u * mixed.reshape(B, S, SG_WIDTH)
    return y @ w_out


def channel_mixer(h, w1, w2):
    a = jax.nn.relu(h @ w1)
    return (a * a) @ w2


def setup_inputs(seed: int = 0) -> dict:
    key = jax.random.key(seed)
    ks = jax.random.split(key, 32)
    f32 = jnp.float32

    def nrm(k, shape, scale):
        return jax.random.normal(k, shape, f32) * scale

    def gain(k, shape):
        return 1.0 + 0.05 * jax.random.normal(k, shape, f32)

    x = nrm(ks[0], (BATCH, SEQ, D_MODEL), 1.0)
    p = nrm(ks[1], (DEPTH, BATCH, SEQ, PLE_DIM), 1.0)
    start = jax.random.randint(ks[2], (BATCH, 1), 0, 65536, dtype=jnp.int32)
    positions = start + jnp.arange(SEQ, dtype=jnp.int32)[None, :]
    return {
        'x': x,
        'p': p,
        'positions': positions,
        'ev_norm_pre': gain(ks[3], (N_EVEN, D_MODEL)),
        'ev_w_in': nrm(ks[4], (N_EVEN, D_MODEL, EVEN_IN_WIDTH), D_MODEL ** -0.5),
        'ev_lam_q1': nrm(ks[5], (N_EVEN, HEAD_DIM), 0.1),
        'ev_lam_k1': nrm(ks[6], (N_EVEN, HEAD_DIM), 0.1),
        'ev_lam_q2': nrm(ks[7], (N_EVEN, HEAD_DIM), 0.1),
        'ev_lam_k2': nrm(ks[8], (N_EVEN, HEAD_DIM), 0.1),
        'ev_subln': gain(ks[9], (N_EVEN, DIFF_V_DIM)),
        'ev_w_out': nrm(ks[10], (N_EVEN, EVEN_OUT_WIDTH, D_MODEL), EVEN_OUT_WIDTH ** -0.5),
        'ev_norm_post': gain(ks[11], (N_EVEN, D_MODEL)),
        'od_norm_pre': gain(ks[12], (N_ODD, D_MODEL)),
        'od_w_in': nrm(ks[13], (N_ODD, D_MODEL, 2 * SG_WIDTH), D_MODEL ** -0.5),
        'od_ln_g': gain(ks[14], (N_ODD, SG_WIDTH)),
        'od_ln_b': nrm(ks[15], (N_ODD, SG_WIDTH), 0.02),
        'od_w_s': nrm(ks[16], (N_ODD, SG_GROUPS, SG_BLOCK, SG_BLOCK), SG_BLOCK ** -0.5),
        'od_b_s': 1.0 + nrm(ks[17], (N_ODD, SG_GROUPS, SG_BLOCK), 0.1),
        'od_w_out': nrm(ks[18], (N_ODD, SG_WIDTH, D_MODEL), SG_WIDTH ** -0.5),
        'od_norm_post': gain(ks[19], (N_ODD, D_MODEL)),
        'ffn_norm_pre': gain(ks[20], (DEPTH, D_MODEL)),
        'ffn_w1': nrm(ks[21], (DEPTH, D_MODEL, D_FF), D_MODEL ** -0.5),
        'ffn_w2': nrm(ks[22], (DEPTH, D_FF, D_MODEL), D_FF ** -0.5),
        'ffn_norm_post': gain(ks[23], (DEPTH, D_MODEL)),
        'ple_w_proj': nrm(ks[24], (DEPTH, PLE_DIM, D_MODEL), PLE_DIM ** -0.5),
        'ple_w_gate': nrm(ks[25], (DEPTH, D_MODEL, D_MODEL), D_MODEL ** -0.5),
        'ple_norm': gain(ks[26], (DEPTH, D_MODEL)),
    }


def reference(x, p, positions, ev_norm_pre, ev_w_in, ev_lam_q1, ev_lam_k1, ev_lam_q2, ev_lam_k2,
              ev_subln, ev_w_out, ev_norm_post, od_norm_pre, od_w_in, od_ln_g, od_ln_b, od_w_s,
              od_b_s, od_w_out, od_norm_post, ffn_norm_pre, ffn_w1, ffn_w2, ffn_norm_post,
              ple_w_proj, ple_w_gate, ple_norm):
    inv_freq = ROPE_THETA ** (-jnp.arange(0, ROT_DIM, 2, dtype=jnp.float32) / ROT_DIM)
    ang = positions.astype(jnp.float32)[..., None] * inv_freq
    cos = jnp.cos(ang)[:, :, None, :]
    sin = jnp.sin(ang)[:, :, None, :]
    h = x
    for i in range(DEPTH):
        if i % 2 == 0:
            j = i // 2
            lambda_init = 0.8 - 0.6 * math.exp(-0.3 * i)
            m = even_mixer(rms_norm(h, ev_norm_pre[j]), ev_w_in[j], ev_lam_q1[j], ev_lam_k1[j],
                           ev_lam_q2[j], ev_lam_k2[j], ev_subln[j], ev_w_out[j], cos, sin,
                           lambda_init)
            h = h + rms_norm(m, ev_norm_post[j])
        else:
            j = i // 2
            m = odd_mixer(rms_norm(h, od_norm_pre[j]), od_w_in[j], od_ln_g[j], od_ln_b[j],
                          od_w_s[j], od_b_s[j], od_w_out[j])
            h = h + rms_norm(m, od_norm_post[j])
        f = channel_mixer(rms_norm(h, ffn_norm_pre[i]), ffn_w1[i], ffn_w2[i])
        h = h + rms_norm(f, ffn_norm_post[i])
        gate = jax.nn.sigmoid(h @ ple_w_gate[i])
        e = p[i] @ ple_w_proj[i]
        h = h + rms_norm(gate * e, ple_norm[i])
    return h
```

```python
import functools
import math

import jax
import jax.numpy as jnp
from jax import lax
from jax.experimental import pallas as pl
from jax.experimental.pallas import tpu as pltpu

D_MODEL = 2048
BATCH = 8
SEQ = 2048
DEPTH = 2
TOKENS = BATCH * SEQ

CHUNK = 64
HEAD_DIM = 128
SB_HEADS = 8
DIFF_HEADS = 4
DIFF_V_DIM = 2 * HEAD_DIM
ROT_DIM = HEAD_DIM // 4
ROT_HALF = ROT_DIM // 2
ROPE_THETA = 500000.0
SG_BLOCK = 128
SG_GROUPS = 16
SG_GROUP_DIM = 128
SG_WIDTH = SG_GROUPS * SG_GROUP_DIM
D_FF = 4 * D_MODEL
PLE_DIM = 256
SB_WIDTH = SB_HEADS * HEAD_DIM
DIFF_QK_WIDTH = DIFF_HEADS * 2 * HEAD_DIM
DIFF_V_WIDTH = DIFF_HEADS * DIFF_V_DIM
EVEN_IN_WIDTH = 3 * SB_WIDTH + 2 * DIFF_QK_WIDTH + DIFF_V_WIDTH
EPS = 1e-6
ATTN_SCALE = HEAD_DIM ** -0.5

LANES = 128
VMEM_LIMIT_BYTES = 56 * 1024 * 1024

F32 = jnp.float32
BF16 = jnp.bfloat16

PROJ_TM = 1024
PROJ_TN = 1024
ATT_TQ = 256
ATT_BK = 256
OUT_TM = 512
FFN_TM = 512
FFN_TF = 1024
PLE_TM = 512
SG_TM = 512
ROPE_TM = 2048


def _params(*semantics):
    return pltpu.CompilerParams(dimension_semantics=semantics,
                                vmem_limit_bytes=VMEM_LIMIT_BYTES)


def _chunk_of(idx):
    return jnp.right_shift(idx, CHUNK.bit_length() - 1)


def _rms_scale(x):
    return lax.rsqrt(jnp.mean(x * x, axis=-1, keepdims=True) + EPS)


def _rope_table_kernel(pos_ref, freq_ref, cos_ref, sin_ref):
    ang = pos_ref[...].astype(F32) * freq_ref[...]
    lane = lax.broadcasted_iota(jnp.int32, ang.shape, 1)
    c = jnp.cos(ang)
    s = jnp.sin(ang)
    cos_ref[...] = jnp.where(lane < ROT_DIM, c, 1.0)
    sin_ref[...] = jnp.where(lane < ROT_HALF, -s, jnp.where(lane < ROT_DIM, s, 0.0))


def _rope_tables(positions):
    inv_freq = ROPE_THETA ** (-jnp.arange(0, ROT_DIM, 2, dtype=F32) / ROT_DIM)
    freq_row = jnp.concatenate(
        [inv_freq, inv_freq, jnp.zeros((LANES - ROT_DIM,), F32)]).reshape(1, LANES)
    pos = jnp.broadcast_to(positions.reshape(TOKENS, 1), (TOKENS, LANES))
    return pl.pallas_call(
        _rope_table_kernel,
        grid=(TOKENS // ROPE_TM,),
        in_specs=[pl.BlockSpec((ROPE_TM, LANES), lambda i: (i, 0)),
                  pl.BlockSpec((1, LANES), lambda i: (0, 0))],
        out_specs=[pl.BlockSpec((ROPE_TM, LANES), lambda i: (i, 0)),
                   pl.BlockSpec((ROPE_TM, LANES), lambda i: (i, 0))],
        out_shape=[jax.ShapeDtypeStruct((TOKENS, LANES), F32)] * 2,
        compiler_params=_params("arbitrary"),
        name="rope_tables",
    )(pos, freq_row)


def _normalize_rows(x_ref, g_ref, xn_ref):
    x = x_ref[...]
    xn_ref[...] = (x * _rms_scale(x) * g_ref[...]).astype(BF16)


def _even_proj_kernel(x_ref, g_ref, w_ref, cos_ref, sin_ref, o_ref, xn_ref):
    j = pl.program_id(1)

    @pl.when(j == 0)
    def _():
        _normalize_rows(x_ref, g_ref, xn_ref)

    acc = jnp.dot(xn_ref[...], w_ref[...], preferred_element_type=F32)

    sb_q_tile = 0
    df_q_tile = (3 * SB_WIDTH) // PROJ_TN
    df_k_tile = (3 * SB_WIDTH + DIFF_QK_WIDTH) // PROJ_TN

    def rope_store(scale):
        c = cos_ref[...]
        s = sin_ref[...]
        lane = lax.broadcasted_iota(jnp.int32, c.shape, 1)
        first = lane < ROT_HALF
        for g in range(PROJ_TN // LANES):
            a = acc[:, g * LANES:(g + 1) * LANES]
            swapped = jnp.where(first, pltpu.roll(a, LANES - ROT_HALF, 1),
                                pltpu.roll(a, ROT_HALF, 1))
            r = a * c + swapped * s
            if scale != 1.0:
                r = r * scale
            o_ref[:, g * LANES:(g + 1) * LANES] = r.astype(o_ref.dtype)

    @pl.when(j == sb_q_tile)
    def _():
        o_ref[...] = (acc * ATTN_SCALE).astype(o_ref.dtype)

    @pl.when(j == df_q_tile)
    def _():
        rope_store(ATTN_SCALE)

    @pl.when(j == df_k_tile)
    def _():
        rope_store(1.0)

    @pl.when((j != sb_q_tile) & (j != df_q_tile) & (j != df_k_tile))
    def _():
        o_ref[...] = acc.astype(o_ref.dtype)


def _even_proj(h, g, w, cos_t, sin_t):
    n = w.shape[1]
    return pl.pallas_call(
        _even_proj_kernel,
        grid=(TOKENS // PROJ_TM, n // PROJ_TN),
        in_specs=[pl.BlockSpec((PROJ_TM, D_MODEL), lambda i, j: (i, 0)),
                  pl.BlockSpec((1, D_MODEL), lambda i, j: (0, 0)),
                  pl.BlockSpec((D_MODEL, PROJ_TN), lambda i, j: (0, j)),
                  pl.BlockSpec((PROJ_TM, LANES), lambda i, j: (i, 0)),
                  pl.BlockSpec((PROJ_TM, LANES), lambda i, j: (i, 0))],
        out_specs=pl.BlockSpec((PROJ_TM, PROJ_TN), lambda i, j: (i, j)),
        out_shape=jax.ShapeDtypeStruct((TOKENS, n), BF16),
        scratch_shapes=[pltpu.VMEM((PROJ_TM, D_MODEL), BF16)],
        compiler_params=_params("arbitrary", "arbitrary"),
        name="even_in_proj",
    )(h, g, w, cos_t, sin_t)


def _gelu_tanh(x):
    c = math.sqrt(2.0 / math.pi)
    return x * (0.5 * (1.0 + jnp.tanh(c * (x + 0.044715 * (x * x * x)))))


def _odd_proj_kernel(x_ref, g_ref, w_ref, o_ref, xn_ref):
    @pl.when(pl.program_id(1) == 0)
    def _():
        _normalize_rows(x_ref, g_ref, xn_ref)

    acc = jnp.dot(xn_ref[...], w_ref[...], preferred_element_type=F32)
    o_ref[...] = _gelu_tanh(acc).astype(o_ref.dtype)


def _odd_proj(h, g, w):
    n = w.shape[1]
    return pl.pallas_call(
        _odd_proj_kernel,
        grid=(TOKENS // PROJ_TM, n // PROJ_TN),
        in_specs=[pl.BlockSpec((PROJ_TM, D_MODEL), lambda i, j: (i, 0)),
                  pl.BlockSpec((1, D_MODEL), lambda i, j: (0, 0)),
                  pl.BlockSpec((D_MODEL, PROJ_TN), lambda i, j: (0, j))],
        out_specs=pl.BlockSpec((PROJ_TM, PROJ_TN), lambda i, j: (i, j)),
        out_shape=jax.ShapeDtypeStruct((TOKENS, n), BF16),
        scratch_shapes=[pltpu.VMEM((PROJ_TM, D_MODEL), BF16)],
        compiler_params=_params("arbitrary", "arbitrary"),
        name="odd_in_proj",
    )(h, g, w)


def _sb_attn_kernel(q_ref, k_ref, v_ref, o_ref):
    i = pl.program_id(2)
    q = q_ref[...]
    row = lax.broadcasted_iota(jnp.int32, (ATT_BK, ATT_BK), 0)
    col = lax.broadcasted_iota(jnp.int32, (ATT_BK, ATT_BK), 1)
    tri = (row > col).astype(BF16)
    tri2 = jnp.concatenate([tri, tri], axis=0)
    causal = col < row

    def block(j, carry, acc, diag):
        kj = k_ref[pl.ds(j * ATT_BK, ATT_BK), :]
        vj = v_ref[pl.ds(j * ATT_BK, ATT_BK), :]
        z = lax.dot_general(q, kj, (((1,), (1,)), ((), ())), preferred_element_type=F32)
        soft = jnp.log1p(jnp.exp(-jnp.abs(z)))
        log_beta = jnp.minimum(z, 0.0) - soft
        log_keep = -(jnp.maximum(z, 0.0) + soft)
        if diag:
            log_keep = jnp.where(causal, log_keep, 0.0)
        hi = log_keep.astype(BF16)
        lo = (log_keep - hi.astype(F32)).astype(BF16)
        later = jnp.dot(jnp.concatenate([hi, lo], axis=1), tri2, preferred_element_type=F32)
        w = jnp.exp(log_beta + later + carry)
        if diag:
            w = jnp.where(causal, w, 0.0)
        acc = acc + jnp.dot(w.astype(BF16), vj, preferred_element_type=F32)
        carry = carry + jnp.sum(log_keep, axis=-1, keepdims=True)
        return carry, acc

    carry0 = jnp.zeros((ATT_TQ, 1), F32)
    acc0 = jnp.zeros((ATT_TQ, HEAD_DIM), F32)
    carry, acc = block(i, carry0, acc0, True)

    def body(jj, state):
        return block(i - 1 - jj, state[0], state[1], False)

    carry, acc = lax.fori_loop(0, i, body, (carry, acc))
    o_ref[...] = acc.astype(o_ref.dtype)


def _sb_attention(proj3):
    return pl.pallas_call(
        _sb_attn_kernel,
        grid=(BATCH, SB_HEADS, SEQ // ATT_TQ),
        in_specs=[pl.BlockSpec((None, ATT_TQ, HEAD_DIM), lambda b, h, i: (b, i, h)),
                  pl.BlockSpec((None, SEQ, HEAD_DIM), lambda b, h, i: (b, 0, SB_HEADS + h)),
                  pl.BlockSpec((None, SEQ, HEAD_DIM), lambda b, h, i: (b, 0, 2 * SB_HEADS + h))],
        out_specs=pl.BlockSpec((None, ATT_TQ, HEAD_DIM), lambda b, h, i: (b, i, h)),
        out_shape=jax.ShapeDtypeStruct((BATCH, SEQ, SB_WIDTH), BF16),
        compiler_params=_params("arbitrary", "arbitrary", "arbitrary"),
        name="sb_attention",
    )(proj3, proj3, proj3)


def _diff_attn_kernel(lam_ref, g_ref, q1_ref, q2_ref, k1_ref, k2_ref, v_ref, o_ref,
                      *, lambda_init):
    i = pl.program_id(2)
    q1 = q1_ref[...]
    q2 = q2_ref[...]
    row = lax.broadcasted_iota(jnp.int32, (ATT_TQ, ATT_BK), 0)
    col = lax.broadcasted_iota(jnp.int32, (ATT_TQ, ATT_BK), 1)
    visible = _chunk_of(col) <= _chunk_of(row)

    def scores(q, k_ref, j):
        kj = k_ref[pl.ds(j * ATT_BK, ATT_BK), :]
        return lax.dot_general(q, kj, (((1,), (1,)), ((), ())), preferred_element_type=F32)

    def first(q, k_ref, vj):
        z = jnp.where(visible, scores(q, k_ref, i), -jnp.inf)
        m = jnp.max(z, axis=-1, keepdims=True)
        p = jnp.exp(z - m)
        l = jnp.sum(p, axis=-1, keepdims=True)
        acc = jnp.dot(p.astype(BF16), vj, preferred_element_type=F32)
        return m, l, acc

    def update(q, k_ref, j, vj, m, l, acc):
        z = scores(q, k_ref, j)
        m_new = jnp.maximum(m, jnp.max(z, axis=-1, keepdims=True))
        alpha = jnp.exp(m - m_new)
        p = jnp.exp(z - m_new)
        l = l * alpha + jnp.sum(p, axis=-1, keepdims=True)
        acc = acc * alpha + jnp.dot(p.astype(BF16), vj, preferred_element_type=F32)
        return m_new, l, acc

    v_diag = v_ref[pl.ds(i * ATT_BK, ATT_BK), :]
    state = first(q1, k1_ref, v_diag) + first(q2, k2_ref, v_diag)

    def body(jj, st):
        j = i - 1 - jj
        vj = v_ref[pl.ds(j * ATT_BK, ATT_BK), :]
        return (update(q1, k1_ref, j, vj, st[0], st[1], st[2])
                + update(q2, k2_ref, j, vj, st[3], st[4], st[5]))

    m1, l1, acc1, m2, l2, acc2 = lax.fori_loop(0, i, body, state)

    lam_rows = lam_ref[...]
    s1 = jnp.sum(lam_rows[0:1, :] * lam_rows[1:2, :], axis=-1, keepdims=True)
    s2 = jnp.sum(lam_rows[2:3, :] * lam_rows[3:4, :], axis=-1, keepdims=True)
    lam = jnp.exp(s1) - jnp.exp(s2) + lambda_init
    o = acc1 / l1 - lam * (acc2 / l2)
    o = o * _rms_scale(o) * g_ref[...]
    o_ref[...] = (o * (1.0 - lambda_init)).astype(o_ref.dtype)


def _diff_attention(proj3, lam_rows, subln_g, lambda_init):
    q_blk = (3 * SB_WIDTH) // HEAD_DIM
    k_blk = (3 * SB_WIDTH + DIFF_QK_WIDTH) // HEAD_DIM
    v_blk = (3 * SB_WIDTH + 2 * DIFF_QK_WIDTH) // DIFF_V_DIM
    return pl.pallas_call(
        functools.partial(_diff_attn_kernel, lambda_init=lambda_init),
        grid=(BATCH, DIFF_HEADS, SEQ // ATT_TQ),
        in_specs=[pl.BlockSpec((4, HEAD_DIM), lambda b, h, i: (0, 0)),
                  pl.BlockSpec((1, DIFF_V_DIM), lambda b, h, i: (0, 0)),
                  pl.BlockSpec((None, ATT_TQ, HEAD_DIM), lambda b, h, i: (b, i, q_blk + 2 * h)),
                  pl.BlockSpec((None, ATT_TQ, HEAD_DIM), lambda b, h, i: (b, i, q_blk + 2 * h + 1)),
                  pl.BlockSpec((None, SEQ, HEAD_DIM), lambda b, h, i: (b, 0, k_blk + 2 * h)),
                  pl.BlockSpec((None, SEQ, HEAD_DIM), lambda b, h, i: (b, 0, k_blk + 2 * h + 1)),
                  pl.BlockSpec((None, SEQ, DIFF_V_DIM), lambda b, h, i: (b, 0, v_blk + h))],
        out_specs=pl.BlockSpec((None, ATT_TQ, DIFF_V_DIM), lambda b, h, i: (b, i, h)),
        out_shape=jax.ShapeDtypeStruct((BATCH, SEQ, DIFF_V_WIDTH), BF16),
        compiler_params=_params("arbitrary", "arbitrary", "arbitrary"),
        name="diff_attention",
    )(lam_rows, subln_g, proj3, proj3, proj3, proj3, proj3)


def _out_proj_kernel(*refs, n_in):
    a_refs = refs[:n_in]
    w_refs = refs[n_in:2 * n_in]
    g_ref, h_ref, o_ref = refs[2 * n_in:]
    m = jnp.dot(a_refs[0][...], w_refs[0][...], preferred_element_type=F32)
    for a_ref, w_ref in zip(a_refs[1:], w_refs[1:]):
        m = m + jnp.dot(a_ref[...], w_ref[...], preferred_element_type=F32)
    o_ref[...] = h_ref[...] + m * _rms_scale(m) * g_ref[...]


def _out_proj(acts, weights, g, h, name):
    n_in = len(acts)
    in_specs = ([pl.BlockSpec((OUT_TM, a.shape[1]), lambda i: (i, 0)) for a in acts]
                + [pl.BlockSpec(w.shape, lambda i: (0, 0)) for w in weights]
                + [pl.BlockSpec((1, D_MODEL), lambda i: (0, 0)),
                   pl.BlockSpec((OUT_TM, D_MODEL), lambda i: (i, 0))])
    return pl.pallas_call(
        functools.partial(_out_proj_kernel, n_in=n_in),
        grid=(TOKENS // OUT_TM,),
        in_specs=in_specs,
        out_specs=pl.BlockSpec((OUT_TM, D_MODEL), lambda i: (i, 0)),
        out_shape=jax.ShapeDtypeStruct((TOKENS, D_MODEL), F32),
        compiler_params=_params("arbitrary"),
        name=name,
    )(*acts, *weights, g, h)


def _ffn_kernel(h_ref, g1_ref, w1_ref, w2_ref, g2_ref, o_ref, hn_ref):
    f = pl.program_id(1)

    @pl.when(f == 0)
    def _():
        _normalize_rows(h_ref, g1_ref, hn_ref)

    a = jnp.maximum(jnp.dot(hn_ref[...], w1_ref[...], preferred_element_type=F32), 0.0)
    part = jnp.dot((a * a).astype(BF16), w2_ref[...], preferred_element_type=F32)

    @pl.when(f == 0)
    def _():
        o_ref[...] = part

    @pl.when(f != 0)
    def _():
        o_ref[...] += part

    @pl.when(f == pl.num_programs(1) - 1)
    def _():
        m = o_ref[...]
        o_ref[...] = h_ref[...] + m * _rms_scale(m) * g2_ref[...]


def _ffn(h, g1, w1, w2, g2):
    return pl.pallas_call(
        _ffn_kernel,
        grid=(TOKENS // FFN_TM, D_FF // FFN_TF),
        in_specs=[pl.BlockSpec((FFN_TM, D_MODEL), lambda i, f: (i, 0)),
                  pl.BlockSpec((1, D_MODEL), lambda i, f: (0, 0)),
                  pl.BlockSpec((D_MODEL, FFN_TF), lambda i, f: (0, f)),
                  pl.BlockSpec((FFN_TF, D_MODEL), lambda i, f: (f, 0)),
                  pl.BlockSpec((1, D_MODEL), lambda i, f: (0, 0))],
        out_specs=pl.BlockSpec((FFN_TM, D_MODEL), lambda i, f: (i, 0)),
        out_shape=jax.ShapeDtypeStruct((TOKENS, D_MODEL), F32),
        scratch_shapes=[pltpu.VMEM((FFN_TM, D_MODEL), BF16)],
        compiler_params=_params("arbitrary", "arbitrary"),
        name="ffn",
    )(h, g1, w1, w2, g2)


def _ple_kernel(h_ref, p_ref, wg_ref, wp_ref, g_ref, o_ref):
    h = h_ref[...]
    gate = jax.nn.sigmoid(jnp.dot(h.astype(BF16), wg_ref[...], preferred_element_type=F32))
    e = jnp.dot(p_ref[...].astype(BF16), wp_ref[...], preferred_element_type=F32)
    m = gate * e
    o_ref[...] = h + m * _rms_scale(m) * g_ref[...]


def _ple(h, p, wg, wp, g):
    return pl.pallas_call(
        _ple_kernel,
        grid=(TOKENS // PLE_TM,),
        in_specs=[pl.BlockSpec((PLE_TM, D_MODEL), lambda i: (i, 0)),
                  pl.BlockSpec((PLE_TM, PLE_DIM), lambda i: (i, 0)),
                  pl.BlockSpec((D_MODEL, D_MODEL), lambda i: (0, 0)),
                  pl.BlockSpec((PLE_DIM, D_MODEL), lambda i: (0, 0)),
                  pl.BlockSpec((1, D_MODEL), lambda i: (0, 0))],
        out_specs=pl.BlockSpec((PLE_TM, D_MODEL), lambda i: (i, 0)),
        out_shape=jax.ShapeDtypeStruct((TOKENS, D_MODEL), F32),
        compiler_params=_params("arbitrary"),
        name="ple",
    )(h, p, wg, wp, g)


def _sg_kernel(u_ref, v_ref, lng_ref, lnb_ref, ws_ref, bst_ref, y_ref, vn_ref):
    v = v_ref[...].astype(F32)
    mu = jnp.mean(v, axis=-1, keepdims=True)
    vc = v - mu
    inv = lax.rsqrt(jnp.mean(vc * vc, axis=-1, keepdims=True) + EPS)
    vn_ref[...] = (vc * inv * lng_ref[...] + lnb_ref[...]).astype(BF16)

    n_blk = SG_TM // SG_BLOCK
    t_idx = lax.broadcasted_iota(jnp.int32, (SG_BLOCK, SG_BLOCK), 0)
    s_idx = lax.broadcasted_iota(jnp.int32, (SG_BLOCK, SG_BLOCK), 1)
    visible = _chunk_of(s_idx) <= _chunk_of(t_idx)
    bst = bst_ref[...]
    for g in range(SG_GROUPS):
        cols = slice(g * SG_GROUP_DIM, (g + 1) * SG_GROUP_DIM)
        w = jnp.where(visible, ws_ref[g], 0.0).astype(BF16)
        rhs = jnp.concatenate(
            [vn_ref[n * SG_BLOCK:(n + 1) * SG_BLOCK, cols] for n in range(n_blk)], axis=1)
        mixed = jnp.dot(w, rhs, preferred_element_type=F32) + bst[:, g:g + 1]
        for n in range(n_blk):
            rows = slice(n * SG_BLOCK, (n + 1) * SG_BLOCK)
            u = u_ref[rows, cols].astype(F32)
            y_ref[rows, cols] = (u * mixed[:, n * SG_BLOCK:(n + 1) * SG_BLOCK]).astype(y_ref.dtype)


def _spatial_gate(uv, ln_g, ln_b, w_s, b_s_t):
    return pl.pallas_call(
        _sg_kernel,
        grid=(TOKENS // SG_TM,),
        in_specs=[pl.BlockSpec((SG_TM, SG_WIDTH), lambda i: (i, 0)),
                  pl.BlockSpec((SG_TM, SG_WIDTH), lambda i: (i, 1)),
                  pl.BlockSpec((1, SG_WIDTH), lambda i: (0, 0)),
                  pl.BlockSpec((1, SG_WIDTH), lambda i: (0, 0)),
                  pl.BlockSpec((SG_GROUPS, SG_BLOCK, SG_BLOCK), lambda i: (0, 0, 0)),
                  pl.BlockSpec((SG_BLOCK, SG_GROUPS), lambda i: (0, 0))],
        out_specs=pl.BlockSpec((SG_TM, SG_WIDTH), lambda i: (i, 0)),
        out_shape=jax.ShapeDtypeStruct((TOKENS, SG_WIDTH), BF16),
        scratch_shapes=[pltpu.VMEM((SG_TM, SG_WIDTH), BF16)],
        compiler_params=_params("arbitrary"),
        name="spatial_gate",
    )(uv, uv, ln_g, ln_b, w_s, b_s_t)


def _row(v):
    return v.reshape(1, -1)


def kernel(x, p, positions, ev_norm_pre, ev_w_in, ev_lam_q1, ev_lam_k1, ev_lam_q2, ev_lam_k2,
           ev_subln, ev_w_out, ev_norm_post, od_norm_pre, od_w_in, od_ln_g, od_ln_b, od_w_s,
           od_b_s, od_w_out, od_norm_post, ffn_norm_pre, ffn_w1, ffn_w2, ffn_norm_post,
           ple_w_proj, ple_w_gate, ple_norm):
    h = x.reshape(TOKENS, D_MODEL)
    p2 = p.reshape(DEPTH, TOKENS, PLE_DIM)
    cos_t, sin_t = _rope_tables(positions)

    for i in range(DEPTH):
        j = i // 2
        if i % 2 == 0:
            lambda_init = 0.8 - 0.6 * math.exp(-0.3 * i)
            proj = _even_proj(h, _row(ev_norm_pre[j]), ev_w_in[j].astype(BF16), cos_t, sin_t)
            proj3 = proj.reshape(BATCH, SEQ, EVEN_IN_WIDTH)
            sb_o = _sb_attention(proj3).reshape(TOKENS, SB_WIDTH)
            lam_rows = jnp.stack([ev_lam_q1[j], ev_lam_k1[j], ev_lam_q2[j], ev_lam_k2[j]])
            df_o = _diff_attention(proj3, lam_rows, _row(ev_subln[j]), lambda_init)
            df_o = df_o.reshape(TOKENS, DIFF_V_WIDTH)
            w_out = ev_w_out[j].astype(BF16)
            h = _out_proj([sb_o, df_o], [w_out[:SB_WIDTH], w_out[SB_WIDTH:]],
                          _row(ev_norm_post[j]), h, "even_out_proj")
        else:
            uv = _odd_proj(h, _row(od_norm_pre[j]), od_w_in[j].astype(BF16))
            y = _spatial_gate(uv, _row(od_ln_g[j]), _row(od_ln_b[j]), od_w_s[j], od_b_s[j].T)
            h = _out_proj([y], [od_w_out[j].astype(BF16)], _row(od_norm_post[j]), h,
                          "odd_out_proj")
        h = _ffn(h, _row(ffn_norm_pre[i]), ffn_w1[i].astype(BF16), ffn_w2[i].astype(BF16),
                 _row(ffn_norm_post[i]))
        h = _ple(h, p2[i], ple_w_gate[i].astype(BF16), ple_w_proj[i].astype(BF16),
                 _row(ple_norm[i]))
    return h.reshape(BATCH, SEQ, D_MODEL)
```

```python
import functools
import math

import jax
import jax.numpy as jnp
from jax import lax
from jax.experimental import pallas as pl
from jax.experimental.pallas import tpu as pltpu

D_MODEL = 2048
BATCH = 8
SEQ = 2048
DEPTH = 2
TOKENS = BATCH * SEQ

CHUNK = 64
HEAD_DIM = 128
SB_HEADS = 8
DIFF_HEADS = 4
DIFF_MAPS = 2 * DIFF_HEADS
DIFF_V_DIM = 2 * HEAD_DIM
ROT_DIM = HEAD_DIM // 4
ROT_HALF = ROT_DIM // 2
ROPE_THETA = 500000.0
SG_BLOCK = 128
SG_GROUPS = 16
SG_GROUP_DIM = 128
SG_WIDTH = SG_GROUPS * SG_GROUP_DIM
D_FF = 4 * D_MODEL
PLE_DIM = 256
SB_WIDTH = SB_HEADS * HEAD_DIM
DIFF_QK_WIDTH = DIFF_HEADS * 2 * HEAD_DIM
DIFF_V_WIDTH = DIFF_HEADS * DIFF_V_DIM
EVEN_IN_WIDTH = 3 * SB_WIDTH + 2 * DIFF_QK_WIDTH + DIFF_V_WIDTH
EPS = 1e-6
LOG2E = math.log2(math.e)
Q_SCALE = HEAD_DIM ** -0.5 * LOG2E
EXP2_ZERO_BELOW = -151.0

LANES = 128
VMEM_LIMIT_BYTES = 56 * 1024 * 1024

F32 = jnp.float32
BF16 = jnp.bfloat16

PROJ_TM = 1024
PROJ_TN = 1024
ATT_TQ = 256
ATT_BK = 256
SB_NH = 4
OUT_TM = 512
FFN_TM = 512
FFN_TF = 1024
PLE_TM = 512
SG_TM = 512
ROPE_TM = 2048

PLAIN_W_TILES = (0, 1, 2, 5)
ROPE_W_TILE0 = 3


def _params(*semantics):
    return pltpu.CompilerParams(dimension_semantics=semantics,
                                vmem_limit_bytes=VMEM_LIMIT_BYTES)


def _chunk_of(idx):
    return jnp.right_shift(idx, CHUNK.bit_length() - 1)


def _rms_scale(x):
    return lax.rsqrt(jnp.mean(x * x, axis=-1, keepdims=True) + EPS)


def _rope_table_kernel(pos_ref, freq_ref, cos_ref, sin_ref):
    ang = pos_ref[...].astype(F32) * freq_ref[...]
    lane = lax.broadcasted_iota(jnp.int32, ang.shape, 1)
    c = jnp.where(lane < ROT_DIM, jnp.cos(ang), 1.0)
    s = jnp.sin(ang)
    s = jnp.where(lane < ROT_HALF, -s, jnp.where(lane < ROT_DIM, s, 0.0))
    cos_ref[0] = c * Q_SCALE
    sin_ref[0] = s * Q_SCALE
    cos_ref[1] = c
    sin_ref[1] = s


def _rope_tables(positions):
    inv_freq = ROPE_THETA ** (-jnp.arange(0, ROT_DIM, 2, dtype=F32) / ROT_DIM)
    freq_row = jnp.concatenate(
        [inv_freq, inv_freq, jnp.zeros((LANES - ROT_DIM,), F32)]).reshape(1, LANES)
    pos = jnp.broadcast_to(positions.reshape(TOKENS, 1), (TOKENS, LANES))
    return pl.pallas_call(
        _rope_table_kernel,
        grid=(TOKENS // ROPE_TM,),
        in_specs=[pl.BlockSpec((ROPE_TM, LANES), lambda i: (i, 0)),
                  pl.BlockSpec((1, LANES), lambda i: (0, 0))],
        out_specs=[pl.BlockSpec((2, ROPE_TM, LANES), lambda i: (0, i, 0)),
                   pl.BlockSpec((2, ROPE_TM, LANES), lambda i: (0, i, 0))],
        out_shape=[jax.ShapeDtypeStruct((2, TOKENS, LANES), F32)] * 2,
        compiler_params=_params("arbitrary"),
        name="rope_tables",
    )(pos, freq_row)


def _normalize_rows(x_ref, g_ref, xn_ref):
    x = x_ref[...]
    xn_ref[...] = (x * _rms_scale(x) * g_ref[...]).astype(BF16)


def _plain_proj_kernel(x_ref, g_ref, w_ref, cs_ref, o_ref, xn_ref):
    @pl.when(pl.program_id(1) == 0)
    def _():
        _normalize_rows(x_ref, g_ref, xn_ref)

    acc = jnp.dot(xn_ref[...], w_ref[...], preferred_element_type=F32)
    o_ref[...] = (acc * cs_ref[...]).astype(o_ref.dtype)


def _plain_proj(h, g, w_stack, layer, col_scale):
    n_tiles = len(PLAIN_W_TILES)
    first_gap = PLAIN_W_TILES.index(5)

    def w_map(i, j):
        return (layer, 0, jnp.where(j < first_gap, j, j + (5 - first_gap)))

    return pl.pallas_call(
        _plain_proj_kernel,
        grid=(TOKENS // PROJ_TM, n_tiles),
        in_specs=[pl.BlockSpec((PROJ_TM, D_MODEL), lambda i, j: (i, 0)),
                  pl.BlockSpec((1, D_MODEL), lambda i, j: (0, 0)),
                  pl.BlockSpec((None, D_MODEL, PROJ_TN), w_map),
                  pl.BlockSpec((1, PROJ_TN), lambda i, j: (0, j))],
        out_specs=[pl.BlockSpec((PROJ_TM, PROJ_TN), lambda i, j: (i, j)),
                   pl.BlockSpec((PROJ_TM, D_MODEL), lambda i, j: (i, 0))],
        out_shape=[jax.ShapeDtypeStruct((TOKENS, n_tiles * PROJ_TN), BF16),
                   jax.ShapeDtypeStruct((TOKENS, D_MODEL), BF16)],
        compiler_params=_params("arbitrary", "arbitrary"),
        name="even_plain_proj",
    )(h, g, w_stack, col_scale)


def _rope_proj_kernel(xn_ref, w_ref, cos_ref, sin_ref, o_ref):
    acc = jnp.dot(xn_ref[...], w_ref[...], preferred_element_type=F32)
    c = cos_ref[...]
    s = sin_ref[...]
    first = lax.broadcasted_iota(jnp.int32, c.shape, 1) < ROT_HALF
    for g in range(PROJ_TN // LANES):
        a = acc[:, g * LANES:(g + 1) * LANES]
        swapped = jnp.where(first, pltpu.roll(a, LANES - ROT_HALF, 1), pltpu.roll(a, ROT_HALF, 1))
        o_ref[:, g * LANES:(g + 1) * LANES] = (a * c + swapped * s).astype(o_ref.dtype)


def _rope_proj(xn, w_stack, layer, cos_t, sin_t):
    return pl.pallas_call(
        _rope_proj_kernel,
        grid=(TOKENS // PROJ_TM, 2),
        in_specs=[pl.BlockSpec((PROJ_TM, D_MODEL), lambda i, j: (i, 0)),
                  pl.BlockSpec((None, D_MODEL, PROJ_TN), lambda i, j: (layer, 0, ROPE_W_TILE0 + j)),
                  pl.BlockSpec((None, PROJ_TM, LANES), lambda i, j: (j, i, 0)),
                  pl.BlockSpec((None, PROJ_TM, LANES), lambda i, j: (j, i, 0))],
        out_specs=pl.BlockSpec((PROJ_TM, PROJ_TN), lambda i, j: (i, j)),
        out_shape=jax.ShapeDtypeStruct((TOKENS, 2 * PROJ_TN), BF16),
        compiler_params=_params("arbitrary", "arbitrary"),
        name="even_rope_proj",
    )(xn, w_stack, cos_t, sin_t)


def _gelu_tanh(x):
    c = math.sqrt(2.0 / math.pi)
    return x * (0.5 * (1.0 + jnp.tanh(c * (x + 0.044715 * (x * x * x)))))


def _odd_proj_kernel(x_ref, g_ref, w_ref, o_ref, xn_ref):
    @pl.when(pl.program_id(1) == 0)
    def _():
        _normalize_rows(x_ref, g_ref, xn_ref)

    acc = jnp.dot(xn_ref[...], w_ref[...], preferred_element_type=F32)
    o_ref[...] = _gelu_tanh(acc).astype(o_ref.dtype)


def _odd_proj(h, g, w_stack, layer):
    n = w_stack.shape[2]
    return pl.pallas_call(
        _odd_proj_kernel,
        grid=(TOKENS // PROJ_TM, n // PROJ_TN),
        in_specs=[pl.BlockSpec((PROJ_TM, D_MODEL), lambda i, j: (i, 0)),
                  pl.BlockSpec((1, D_MODEL), lambda i, j: (0, 0)),
                  pl.BlockSpec((None, D_MODEL, PROJ_TN), lambda i, j: (layer, 0, j))],
        out_specs=pl.BlockSpec((PROJ_TM, PROJ_TN), lambda i, j: (i, j)),
        out_shape=jax.ShapeDtypeStruct((TOKENS, n), BF16),
        scratch_shapes=[pltpu.VMEM((PROJ_TM, D_MODEL), BF16)],
        compiler_params=_params("arbitrary", "arbitrary"),
        name="odd_in_proj",
    )(h, g, w_stack)


def _store_transposed_values(v_ref, vt_ref):
    for jb in range(SEQ // ATT_BK):
        vt_ref[jb] = v_ref[jb * ATT_BK:(jb + 1) * ATT_BK, :].astype(F32).T.astype(BF16)


def _scores_t(k_ref, q_ref, j, cols):
    kj = k_ref[pl.ds(j * ATT_BK, ATT_BK), cols]
    return lax.dot_general(kj, q_ref[:, cols], (((1,), (1,)), ((), ())),
                           preferred_element_type=F32)


def _sb_attn_kernel(q_ref, k_ref, v_ref, o_ref, vt_ref, acc_ref, carry_ref):
    i = pl.program_id(2)

    @pl.when(i == 0)
    def _():
        _store_transposed_values(v_ref, vt_ref)

    key = lax.broadcasted_iota(jnp.int32, (ATT_BK, ATT_TQ), 0)
    qry = lax.broadcasted_iota(jnp.int32, (ATT_BK, ATT_TQ), 1)
    causal = key < qry
    later_key = lax.broadcasted_iota(jnp.int32, (ATT_BK, ATT_BK), 1)
    this_key = lax.broadcasted_iota(jnp.int32, (ATT_BK, ATT_BK), 0)
    tri = (later_key > this_key).astype(BF16)
    tri2 = jnp.concatenate([tri, tri], axis=1)

    heads = range(SB_NH)

    def head_cols(h):
        return slice(h * HEAD_DIM, (h + 1) * HEAD_DIM)

    def blocks(j, carries, diag):
        zs = [_scores_t(k_ref, q_ref, j, head_cols(h)) for h in heads]
        log_betas, log_keeps, splits = [], [], []
        for z in zs:
            soft = jnp.log2(1.0 + jnp.exp2(-jnp.abs(z)))
            log_beta = jnp.minimum(z, 0.0) - soft
            log_keep = log_beta - z
            if diag:
                log_keep = jnp.where(causal, log_keep, 0.0)
            hi = log_keep.astype(BF16)
            lo = (log_keep - hi.astype(F32)).astype(BF16)
            log_betas.append(log_beta)
            log_keeps.append(log_keep)
            splits.append(jnp.concatenate([hi, lo], axis=0))
        laters = [jnp.dot(tri2, s, preferred_element_type=F32) for s in splits]
        ws = []
        for h in heads:
            arg = log_betas[h] + laters[h]
            if carries is not None:
                arg = arg + carries[h]
            w = jnp.exp2(arg)
            if diag:
                w = jnp.where(causal, w, 0.0)
            ws.append(w.astype(BF16))
        contribs = [jnp.dot(vt_ref[j, head_cols(h), :], ws[h], preferred_element_type=F32)
                    for h in heads]
        colsums = [jnp.sum(lk, axis=0, keepdims=True) for lk in log_keeps]
        return contribs, colsums

    def any_weight_left(carries):
        worst = functools.reduce(jnp.maximum, carries)
        return (jnp.max(worst) > EXP2_ZERO_BELOW).astype(jnp.int32)

    contribs, colsums = blocks(i, None, True)
    for h in heads:
        acc_ref[h] = contribs[h]
        carry_ref[h] = colsums[h]

    def cond(state):
        j, more = state
        return jnp.logical_and(j >= 0, more > 0)

    def body(state):
        j, _ = state
        carries = [carry_ref[h] for h in heads]
        contribs, colsums = blocks(j, carries, False)
        carries = [carries[h] + colsums[h] for h in heads]
        for h in heads:
            acc_ref[h] += contribs[h]
            carry_ref[h] = carries[h]
        return j - 1, any_weight_left(carries)

    lax.while_loop(cond, body, (i - 1, any_weight_left(colsums)))
    for h in range(SB_NH):
        o_ref[:, h * HEAD_DIM:(h + 1) * HEAD_DIM] = acc_ref[h].T.astype(o_ref.dtype)


def _sb_attention(proj3):
    width = SB_NH * HEAD_DIM
    groups = SB_HEADS // SB_NH
    return pl.pallas_call(
        _sb_attn_kernel,
        grid=(BATCH, groups, SEQ // ATT_TQ),
        in_specs=[pl.BlockSpec((None, ATT_TQ, width), lambda b, g, i: (b, i, g)),
                  pl.BlockSpec((None, SEQ, width), lambda b, g, i: (b, 0, groups + g)),
                  pl.BlockSpec((None, SEQ, width), lambda b, g, i: (b, 0, 2 * groups + g))],
        out_specs=pl.BlockSpec((None, ATT_TQ, width), lambda b, g, i: (b, i, g)),
        out_shape=jax.ShapeDtypeStruct((BATCH, SEQ, SB_WIDTH), BF16),
        scratch_shapes=[pltpu.VMEM((SEQ // ATT_BK, width, ATT_BK), BF16),
                        pltpu.VMEM((SB_NH, HEAD_DIM, ATT_TQ), F32),
                        pltpu.VMEM((SB_NH, 1, ATT_TQ), F32)],
        compiler_params=_params("arbitrary", "arbitrary", "arbitrary"),
        name="sb_attention",
    )(proj3, proj3, proj3)


def _diff_attn_kernel(lam_ref, g_ref, q_ref, k_ref, v_ref, o_ref, vt_ref, acc_ref, m_ref, l_ref,
                      *, lambda_init):
    i = pl.program_id(1)

    @pl.when(i == 0)
    def _():
        _store_transposed_values(v_ref, vt_ref)

    key = lax.broadcasted_iota(jnp.int32, (ATT_BK, ATT_TQ), 0)
    qry = lax.broadcasted_iota(jnp.int32, (ATT_BK, ATT_TQ), 1)
    visible = _chunk_of(key) <= _chunk_of(qry)

    def head_cols(c):
        return slice(c * HEAD_DIM, (c + 1) * HEAD_DIM)

    def value_rows(c):
        return slice((c // 2) * DIFF_V_DIM, (c // 2 + 1) * DIFF_V_DIM)

    maps = range(DIFF_MAPS)

    zs = [jnp.where(visible, _scores_t(k_ref, q_ref, i, head_cols(c)), -jnp.inf) for c in maps]
    ps = []
    for c in maps:
        m = jnp.max(zs[c], axis=0, keepdims=True)
        p = jnp.exp2(zs[c] - m)
        m_ref[c] = m
        l_ref[c] = jnp.sum(p, axis=0, keepdims=True)
        ps.append(p.astype(BF16))
    for c in maps:
        acc_ref[c] = jnp.dot(vt_ref[i, value_rows(c), :], ps[c],
                             preferred_element_type=F32)

    def body(jj, unused):
        j = i - 1 - jj
        zs = [_scores_t(k_ref, q_ref, j, head_cols(c)) for c in maps]
        ps, alphas = [], []
        for c in maps:
            m_old = m_ref[c]
            m_new = jnp.maximum(m_old, jnp.max(zs[c], axis=0, keepdims=True))
            alpha = jnp.exp2(m_old - m_new)
            p = jnp.exp2(zs[c] - m_new)
            m_ref[c] = m_new
            l_ref[c] = l_ref[c] * alpha + jnp.sum(p, axis=0, keepdims=True)
            ps.append(p.astype(BF16))
            alphas.append(alpha)
        pvs = [jnp.dot(vt_ref[j, value_rows(c), :], ps[c], preferred_element_type=F32)
               for c in maps]
        for c in maps:
            acc_ref[c] = acc_ref[c] * alphas[c] + pvs[c]
        return unused

    lax.fori_loop(0, i, body, 0)

    lam_rows = lam_ref[...]
    s1 = jnp.sum(lam_rows[0:1, :] * lam_rows[1:2, :], axis=-1, keepdims=True)
    s2 = jnp.sum(lam_rows[2:3, :] * lam_rows[3:4, :], axis=-1, keepdims=True)
    lam = jnp.exp(s1) - jnp.exp(s2) + lambda_init
    gain = g_ref[...]
    for h in range(DIFF_HEADS):
        o = acc_ref[2 * h] / l_ref[2 * h] - lam * (acc_ref[2 * h + 1] / l_ref[2 * h + 1])
        r = lax.rsqrt(jnp.mean(o * o, axis=0, keepdims=True) + EPS)
        o = o * r * gain * (1.0 - lambda_init)
        o_ref[:, h * DIFF_V_DIM:(h + 1) * DIFF_V_DIM] = o.T.astype(o_ref.dtype)


def _diff_attention(qk3, proj3, lam_rows, subln_col, lambda_init):
    v_tile = PLAIN_W_TILES.index(5)
    return pl.pallas_call(
        functools.partial(_diff_attn_kernel, lambda_init=lambda_init),
        grid=(BATCH, SEQ // ATT_TQ),
        in_specs=[pl.BlockSpec((4, HEAD_DIM), lambda b, i: (0, 0)),
                  pl.BlockSpec((DIFF_V_DIM, 1), lambda b, i: (0, 0)),
                  pl.BlockSpec((None, ATT_TQ, DIFF_QK_WIDTH), lambda b, i: (b, i, 0)),
                  pl.BlockSpec((None, SEQ, DIFF_QK_WIDTH), lambda b, i: (b, 0, 1)),
                  pl.BlockSpec((None, SEQ, DIFF_V_WIDTH), lambda b, i: (b, 0, v_tile))],
        out_specs=pl.BlockSpec((None, ATT_TQ, DIFF_V_WIDTH), lambda b, i: (b, i, 0)),
        out_shape=jax.ShapeDtypeStruct((BATCH, SEQ, DIFF_V_WIDTH), BF16),
        scratch_shapes=[pltpu.VMEM((SEQ // ATT_BK, DIFF_V_WIDTH, ATT_BK), BF16),
                        pltpu.VMEM((DIFF_MAPS, DIFF_V_DIM, ATT_TQ), F32),
                        pltpu.VMEM((DIFF_MAPS, 1, ATT_TQ), F32),
                        pltpu.VMEM((DIFF_MAPS, 1, ATT_TQ), F32)],
        compiler_params=_params("arbitrary", "arbitrary"),
        name="diff_attention",
    )(lam_rows, subln_col, qk3, qk3, proj3)


def _out_proj_kernel(*refs, n_in):
    a_refs = refs[:n_in]
    w_refs = refs[n_in:2 * n_in]
    g_ref, h_ref, o_ref = refs[2 * n_in:]
    m = jnp.dot(a_refs[0][...], w_refs[0][...], preferred_element_type=F32)
    for a_ref, w_ref in zip(a_refs[1:], w_refs[1:]):
        m = m + jnp.dot(a_ref[...], w_ref[...], preferred_element_type=F32)
    o_ref[...] = h_ref[...] + m * _rms_scale(m) * g_ref[...]


def _out_proj(acts, w_stack, layer, g, h, name):
    n_in = len(acts)
    k = w_stack.shape[1] // n_in
    in_specs = ([pl.BlockSpec((OUT_TM, k), lambda i: (i, 0)) for _ in acts]
                + [pl.BlockSpec((None, k, D_MODEL), functools.partial(lambda i, r: (layer, r, 0), r=r))
                   for r in range(n_in)]
                + [pl.BlockSpec((1, D_MODEL), lambda i: (0, 0)),
                   pl.BlockSpec((OUT_TM, D_MODEL), lambda i: (i, 0))])
    return pl.pallas_call(
        functools.partial(_out_proj_kernel, n_in=n_in),
        grid=(TOKENS // OUT_TM,),
        in_specs=in_specs,
        out_specs=pl.BlockSpec((OUT_TM, D_MODEL), lambda i: (i, 0)),
        out_shape=jax.ShapeDtypeStruct((TOKENS, D_MODEL), F32),
        compiler_params=_params("arbitrary"),
        name=name,
    )(*acts, *([w_stack] * n_in), g, h)


def _ffn_kernel(h_ref, g1_ref, w1_ref, w2_ref, g2_ref, o_ref, hn_ref):
    f = pl.program_id(1)

    @pl.when(f == 0)
    def _():
        _normalize_rows(h_ref, g1_ref, hn_ref)
        o_ref[...] = jnp.zeros_like(o_ref)

    a = jnp.maximum(jnp.dot(hn_ref[...], w1_ref[...], preferred_element_type=F32), 0.0)
    o_ref[...] += jnp.dot((a * a).astype(BF16), w2_ref[...], preferred_element_type=F32)

    @pl.when(f == pl.num_programs(1) - 1)
    def _():
        m = o_ref[...]
        o_ref[...] = h_ref[...] + m * _rms_scale(m) * g2_ref[...]


def _ffn(h, g1, w1_stack, w2_stack, layer, g2):
    return pl.pallas_call(
        _ffn_kernel,
        grid=(TOKENS // FFN_TM, D_FF // FFN_TF),
        in_specs=[pl.BlockSpec((FFN_TM, D_MODEL), lambda i, f: (i, 0)),
                  pl.BlockSpec((1, D_MODEL), lambda i, f: (0, 0)),
                  pl.BlockSpec((None, D_MODEL, FFN_TF), lambda i, f: (layer, 0, f)),
                  pl.BlockSpec((None, FFN_TF, D_MODEL), lambda i, f: (layer, f, 0)),
                  pl.BlockSpec((1, D_MODEL), lambda i, f: (0, 0))],
        out_specs=pl.BlockSpec((FFN_TM, D_MODEL), lambda i, f: (i, 0)),
        out_shape=jax.ShapeDtypeStruct((TOKENS, D_MODEL), F32),
        scratch_shapes=[pltpu.VMEM((FFN_TM, D_MODEL), BF16)],
        compiler_params=_params("arbitrary", "arbitrary"),
        name="ffn",
    )(h, g1, w1_stack, w2_stack, g2)


def _ple_kernel(h_ref, p_ref, wg_ref, wp_ref, g_ref, o_ref):
    h = h_ref[...]
    gate = jax.nn.sigmoid(jnp.dot(h.astype(BF16), wg_ref[...], preferred_element_type=F32))
    e = jnp.dot(p_ref[...].astype(BF16), wp_ref[...], preferred_element_type=F32)
    m = gate * e
    o_ref[...] = h + m * _rms_scale(m) * g_ref[...]


def _ple(h, p_stack, wg_stack, wp_stack, layer, g):
    return pl.pallas_call(
        _ple_kernel,
        grid=(TOKENS // PLE_TM,),
        in_specs=[pl.BlockSpec((PLE_TM, D_MODEL), lambda i: (i, 0)),
                  pl.BlockSpec((None, PLE_TM, PLE_DIM), lambda i: (layer, i, 0)),
                  pl.BlockSpec((None, D_MODEL, D_MODEL), lambda i: (layer, 0, 0)),
                  pl.BlockSpec((None, PLE_DIM, D_MODEL), lambda i: (layer, 0, 0)),
                  pl.BlockSpec((1, D_MODEL), lambda i: (0, 0))],
        out_specs=pl.BlockSpec((PLE_TM, D_MODEL), lambda i: (i, 0)),
        out_shape=jax.ShapeDtypeStruct((TOKENS, D_MODEL), F32),
        compiler_params=_params("arbitrary"),
        name="ple",
    )(h, p_stack, wg_stack, wp_stack, g)


def _sg_kernel(u_ref, v_ref, lng_ref, lnb_ref, ws_ref, bst_ref, y_ref, vn_ref):
    v = v_ref[...].astype(F32)
    mu = jnp.mean(v, axis=-1, keepdims=True)
    vc = v - mu
    inv = lax.rsqrt(jnp.mean(vc * vc, axis=-1, keepdims=True) + EPS)
    vn_ref[...] = (vc * inv * lng_ref[...] + lnb_ref[...]).astype(BF16)

    n_blk = SG_TM // SG_BLOCK
    t_idx = lax.broadcasted_iota(jnp.int32, (SG_BLOCK, SG_BLOCK), 0)
    s_idx = lax.broadcasted_iota(jnp.int32, (SG_BLOCK, SG_BLOCK), 1)
    visible = _chunk_of(s_idx) <= _chunk_of(t_idx)
    bst = bst_ref[...]
    for g in range(SG_GROUPS):
        cols = slice(g * SG_GROUP_DIM, (g + 1) * SG_GROUP_DIM)
        w = jnp.where(visible, ws_ref[g], 0.0).astype(BF16)
        rhs = jnp.concatenate(
            [vn_ref[n * SG_BLOCK:(n + 1) * SG_BLOCK, cols] for n in range(n_blk)], axis=1)
        mixed = jnp.dot(w, rhs, preferred_element_type=F32) + bst[:, g:g + 1]
        for n in range(n_blk):
            rows = slice(n * SG_BLOCK, (n + 1) * SG_BLOCK)
            u = u_ref[rows, cols].astype(F32)
            y_ref[rows, cols] = (u * mixed[:, n * SG_BLOCK:(n + 1) * SG_BLOCK]).astype(y_ref.dtype)


def _spatial_gate(uv, ln_g, ln_b, w_s_stack, layer, b_s_t):
    return pl.pallas_call(
        _sg_kernel,
        grid=(TOKENS // SG_TM,),
        in_specs=[pl.BlockSpec((SG_TM, SG_WIDTH), lambda i: (i, 0)),
                  pl.BlockSpec((SG_TM, SG_WIDTH), lambda i: (i, 1)),
                  pl.BlockSpec((1, SG_WIDTH), lambda i: (0, 0)),
                  pl.BlockSpec((1, SG_WIDTH), lambda i: (0, 0)),
                  pl.BlockSpec((None, SG_GROUPS, SG_BLOCK, SG_BLOCK), lambda i: (layer, 0, 0, 0)),
                  pl.BlockSpec((SG_BLOCK, SG_GROUPS), lambda i: (0, 0))],
        out_specs=pl.BlockSpec((SG_TM, SG_WIDTH), lambda i: (i, 0)),
        out_shape=jax.ShapeDtypeStruct((TOKENS, SG_WIDTH), BF16),
        scratch_shapes=[pltpu.VMEM((SG_TM, SG_WIDTH), BF16)],
        compiler_params=_params("arbitrary"),
        name="spatial_gate",
    )(uv, uv, ln_g, ln_b, w_s_stack, b_s_t)


def _row(v):
    return v.reshape(1, -1)


def kernel(x, p, positions, ev_norm_pre, ev_w_in, ev_lam_q1, ev_lam_k1, ev_lam_q2, ev_lam_k2,
           ev_subln, ev_w_out, ev_norm_post, od_norm_pre, od_w_in, od_ln_g, od_ln_b, od_w_s,
           od_b_s, od_w_out, od_norm_post, ffn_norm_pre, ffn_w1, ffn_w2, ffn_norm_post,
           ple_w_proj, ple_w_gate, ple_norm):
    h = x.reshape(TOKENS, D_MODEL)
    p_stack = p.reshape(DEPTH, TOKENS, PLE_DIM)
    cos_t, sin_t = _rope_tables(positions)
    ev_w_in, ev_w_out, od_w_in, od_w_out, ffn_w1, ffn_w2, ple_w_proj, ple_w_gate = (
        w.astype(BF16) for w in
        (ev_w_in, ev_w_out, od_w_in, od_w_out, ffn_w1, ffn_w2, ple_w_proj, ple_w_gate))
    sb_q_scale = jnp.concatenate([jnp.full((1, SB_WIDTH), Q_SCALE, F32),
                                  jnp.ones((1, (len(PLAIN_W_TILES) - 1) * PROJ_TN), F32)], axis=1)

    for i in range(DEPTH):
        j = i // 2
        if i % 2 == 0:
            lambda_init = 0.8 - 0.6 * math.exp(-0.3 * i)
            proj, xn = _plain_proj(h, _row(ev_norm_pre[j]), ev_w_in, j, sb_q_scale)
            qk = _rope_proj(xn, ev_w_in, j, cos_t, sin_t)
            proj3 = proj.reshape(BATCH, SEQ, len(PLAIN_W_TILES) * PROJ_TN)
            qk3 = qk.reshape(BATCH, SEQ, 2 * DIFF_QK_WIDTH)
            sb_o = _sb_attention(proj3).reshape(TOKENS, SB_WIDTH)
            lam_rows = jnp.stack([ev_lam_q1[j], ev_lam_k1[j], ev_lam_q2[j], ev_lam_k2[j]])
            df_o = _diff_attention(qk3, proj3, lam_rows, ev_subln[j].reshape(DIFF_V_DIM, 1),
                                   lambda_init).reshape(TOKENS, DIFF_V_WIDTH)
            h = _out_proj([sb_o, df_o], ev_w_out, j, _row(ev_norm_post[j]), h, "even_out_proj")
        else:
            uv = _odd_proj(h, _row(od_norm_pre[j]), od_w_in, j)
            y = _spatial_gate(uv, _row(od_ln_g[j]), _row(od_ln_b[j]), od_w_s, j, od_b_s[j].T)
            h = _out_proj([y], od_w_out, j, _row(od_norm_post[j]), h, "odd_out_proj")
        h = _ffn(h, _row(ffn_norm_pre[i]), ffn_w1, ffn_w2, i, _row(ffn_norm_post[i]))
        h = _ple(h, p_stack, ple_w_gate, ple_w_proj, i, _row(ple_norm[i]))
    return h.reshape(BATCH, SEQ, D_MODEL)
```

```python
import functools
import math

import jax
import jax.numpy as jnp
from jax import lax
from jax.experimental import pallas as pl
from jax.experimental.pallas import tpu as pltpu

D_MODEL = 2048
BATCH = 8
SEQ = 2048
DEPTH = 2
TOKENS = BATCH * SEQ

CHUNK = 64
HEAD_DIM = 128
SB_HEADS = 8
DIFF_HEADS = 4
DIFF_MAPS = 2 * DIFF_HEADS
DIFF_V_DIM = 2 * HEAD_DIM
ROT_DIM = HEAD_DIM // 4
ROT_HALF = ROT_DIM // 2
ROPE_THETA = 500000.0
SG_BLOCK = 128
SG_GROUPS = 16
SG_GROUP_DIM = 128
SG_WIDTH = SG_GROUPS * SG_GROUP_DIM
D_FF = 4 * D_MODEL
PLE_DIM = 256
SB_WIDTH = SB_HEADS * HEAD_DIM
DIFF_QK_WIDTH = DIFF_HEADS * 2 * HEAD_DIM
DIFF_V_WIDTH = DIFF_HEADS * DIFF_V_DIM
EVEN_IN_WIDTH = 3 * SB_WIDTH + 2 * DIFF_QK_WIDTH + DIFF_V_WIDTH
EPS = 1e-6
LOG2E = math.log2(math.e)
Q_SCALE = HEAD_DIM ** -0.5 * LOG2E
EXP2_ZERO_BELOW = -151.0

LANES = 128
VMEM_LIMIT_BYTES = 56 * 1024 * 1024

F32 = jnp.float32
BF16 = jnp.bfloat16

PROJ_TM = 1024
PROJ_TN = 1024
ATT_TQ = 256
ATT_BK = 256
SB_NH = 8
OUT_TM = 512
FFN_TM = 1024
FFN_TF = 512
PLE_TM = 512
SG_TM = 512
ROPE_TM = 2048
NORM_ROWS = 16

PLAIN_W_TILES = (0, 1, 2, 5)
ROPE_W_TILE0 = 3


def _params(*semantics):
    return pltpu.CompilerParams(dimension_semantics=semantics,
                                vmem_limit_bytes=VMEM_LIMIT_BYTES)


def _chunk_of(idx):
    return jnp.right_shift(idx, CHUNK.bit_length() - 1)


def _rms_scale(x):
    return lax.rsqrt(jnp.mean(x * x, axis=-1, keepdims=True) + EPS)


def _rope_table_kernel(pos_ref, freq_ref, cos_ref, sin_ref):
    ang = pos_ref[...].astype(F32) * freq_ref[...]
    lane = lax.broadcasted_iota(jnp.int32, ang.shape, 1)
    c = jnp.where(lane < ROT_DIM, jnp.cos(ang), 1.0)
    s = jnp.sin(ang)
    s = jnp.where(lane < ROT_HALF, -s, jnp.where(lane < ROT_DIM, s, 0.0))
    cos_ref[0] = c * Q_SCALE
    sin_ref[0] = s * Q_SCALE
    cos_ref[1] = c
    sin_ref[1] = s


def _rope_tables(positions):
    inv_freq = ROPE_THETA ** (-jnp.arange(0, ROT_DIM, 2, dtype=F32) / ROT_DIM)
    freq_row = jnp.concatenate(
        [inv_freq, inv_freq, jnp.zeros((LANES - ROT_DIM,), F32)]).reshape(1, LANES)
    pos = jnp.broadcast_to(positions.reshape(TOKENS, 1), (TOKENS, LANES))
    return pl.pallas_call(
        _rope_table_kernel,
        grid=(TOKENS // ROPE_TM,),
        in_specs=[pl.BlockSpec((ROPE_TM, LANES), lambda i: (i, 0)),
                  pl.BlockSpec((1, LANES), lambda i: (0, 0))],
        out_specs=[pl.BlockSpec((2, ROPE_TM, LANES), lambda i: (0, i, 0)),
                   pl.BlockSpec((2, ROPE_TM, LANES), lambda i: (0, i, 0))],
        out_shape=[jax.ShapeDtypeStruct((2, TOKENS, LANES), F32)] * 2,
        compiler_params=_params("arbitrary"),
        name="rope_tables",
    )(pos, freq_row)


def _normalize_rows(x_ref, g_ref, xn_ref):
    g = g_ref[...]
    for r0 in range(0, x_ref.shape[0], NORM_ROWS):
        x = x_ref[r0:r0 + NORM_ROWS, :]
        xn_ref[r0:r0 + NORM_ROWS, :] = (x * _rms_scale(x) * g).astype(BF16)


def _add_normalized_rows(h_ref, m_ref, g_ref, o_ref):
    g = g_ref[...]
    for r0 in range(0, h_ref.shape[0], NORM_ROWS):
        rows = slice(r0, r0 + NORM_ROWS)
        m = m_ref[rows, :]
        o_ref[rows, :] = h_ref[rows, :] + m * _rms_scale(m) * g


def _plain_proj_kernel(x_ref, g_ref, w_ref, cs_ref, o_ref, xn_ref):
    @pl.when(pl.program_id(1) == 0)
    def _():
        _normalize_rows(x_ref, g_ref, xn_ref)

    acc = jnp.dot(xn_ref[...], w_ref[...], preferred_element_type=F32)
    o_ref[...] = (acc * cs_ref[...]).astype(o_ref.dtype)


def _plain_proj(h, g, w_stack, layer, col_scale):
    n_tiles = len(PLAIN_W_TILES)
    first_gap = PLAIN_W_TILES.index(5)

    def w_map(i, j):
        return (layer, 0, jnp.where(j < first_gap, j, j + (5 - first_gap)))

    return pl.pallas_call(
        _plain_proj_kernel,
        grid=(TOKENS // PROJ_TM, n_tiles),
        in_specs=[pl.BlockSpec((PROJ_TM, D_MODEL), lambda i, j: (i, 0)),
                  pl.BlockSpec((1, D_MODEL), lambda i, j: (0, 0)),
                  pl.BlockSpec((None, D_MODEL, PROJ_TN), w_map),
                  pl.BlockSpec((1, PROJ_TN), lambda i, j: (0, j))],
        out_specs=[pl.BlockSpec((PROJ_TM, PROJ_TN), lambda i, j: (i, j)),
                   pl.BlockSpec((PROJ_TM, D_MODEL), lambda i, j: (i, 0))],
        out_shape=[jax.ShapeDtypeStruct((TOKENS, n_tiles * PROJ_TN), BF16),
                   jax.ShapeDtypeStruct((TOKENS, D_MODEL), BF16)],
        compiler_params=_params("arbitrary", "arbitrary"),
        name="even_plain_proj",
    )(h, g, w_stack, col_scale)


def _rope_proj_kernel(xn_ref, w_ref, cos_ref, sin_ref, o_ref):
    acc = jnp.dot(xn_ref[...], w_ref[...], preferred_element_type=F32)
    c = cos_ref[...]
    s = sin_ref[...]
    first = lax.broadcasted_iota(jnp.int32, c.shape, 1) < ROT_HALF
    for g in range(PROJ_TN // LANES):
        a = acc[:, g * LANES:(g + 1) * LANES]
        swapped = jnp.where(first, pltpu.roll(a, LANES - ROT_HALF, 1), pltpu.roll(a, ROT_HALF, 1))
        o_ref[:, g * LANES:(g + 1) * LANES] = (a * c + swapped * s).astype(o_ref.dtype)


def _rope_proj(xn, w_stack, layer, cos_t, sin_t):
    return pl.pallas_call(
        _rope_proj_kernel,
        grid=(TOKENS // PROJ_TM, 2),
        in_specs=[pl.BlockSpec((PROJ_TM, D_MODEL), lambda i, j: (i, 0)),
                  pl.BlockSpec((None, D_MODEL, PROJ_TN), lambda i, j: (layer, 0, ROPE_W_TILE0 + j)),
                  pl.BlockSpec((None, PROJ_TM, LANES), lambda i, j: (j, i, 0)),
                  pl.BlockSpec((None, PROJ_TM, LANES), lambda i, j: (j, i, 0))],
        out_specs=pl.BlockSpec((PROJ_TM, PROJ_TN), lambda i, j: (i, j)),
        out_shape=jax.ShapeDtypeStruct((TOKENS, 2 * PROJ_TN), BF16),
        compiler_params=_params("arbitrary", "arbitrary"),
        name="even_rope_proj",
    )(xn, w_stack, cos_t, sin_t)


def _gelu_tanh(x):
    c = -2.0 * math.sqrt(2.0 / math.pi) * LOG2E
    return x / (1.0 + jnp.exp2(x * (c + (0.044715 * c) * (x * x))))


def _odd_proj_kernel(x_ref, g_ref, w_ref, o_ref, xn_ref):
    @pl.when(pl.program_id(1) == 0)
    def _():
        _normalize_rows(x_ref, g_ref, xn_ref)

    acc = jnp.dot(xn_ref[...], w_ref[...], preferred_element_type=F32)
    o_ref[...] = _gelu_tanh(acc).astype(o_ref.dtype)


def _odd_proj(h, g, w_stack, layer):
    n = w_stack.shape[2]
    return pl.pallas_call(
        _odd_proj_kernel,
        grid=(TOKENS // PROJ_TM, n // PROJ_TN),
        in_specs=[pl.BlockSpec((PROJ_TM, D_MODEL), lambda i, j: (i, 0)),
                  pl.BlockSpec((1, D_MODEL), lambda i, j: (0, 0)),
                  pl.BlockSpec((None, D_MODEL, PROJ_TN), lambda i, j: (layer, 0, j))],
        out_specs=pl.BlockSpec((PROJ_TM, PROJ_TN), lambda i, j: (i, j)),
        out_shape=jax.ShapeDtypeStruct((TOKENS, n), BF16),
        scratch_shapes=[pltpu.VMEM((PROJ_TM, D_MODEL), BF16)],
        compiler_params=_params("arbitrary", "arbitrary"),
        name="odd_in_proj",
    )(h, g, w_stack)


def _store_transposed_values(v_ref, vt_ref):
    for jb in range(SEQ // ATT_BK):
        vt_ref[jb] = v_ref[jb * ATT_BK:(jb + 1) * ATT_BK, :].astype(F32).T.astype(BF16)


def _scores_t(k_ref, q_ref, j, cols):
    kj = k_ref[pl.ds(j * ATT_BK, ATT_BK), cols]
    return lax.dot_general(kj, q_ref[:, cols], (((1,), (1,)), ((), ())),
                           preferred_element_type=F32)


def _sb_attn_kernel(q_ref, k_ref, v_ref, o_ref, vt_ref, acc_ref, carry_ref):
    i = pl.program_id(2)

    @pl.when(i == 0)
    def _():
        _store_transposed_values(v_ref, vt_ref)

    key = lax.broadcasted_iota(jnp.int32, (ATT_BK, ATT_TQ), 0)
    qry = lax.broadcasted_iota(jnp.int32, (ATT_BK, ATT_TQ), 1)
    causal = key < qry
    later_key = lax.broadcasted_iota(jnp.int32, (ATT_BK, ATT_BK), 1)
    this_key = lax.broadcasted_iota(jnp.int32, (ATT_BK, ATT_BK), 0)
    tri = (later_key > this_key).astype(BF16)

    heads = range(SB_NH)

    def head_cols(h):
        return slice(h * HEAD_DIM, (h + 1) * HEAD_DIM)

    def blocks(j, carries, diag):
        zs = [_scores_t(k_ref, q_ref, j, head_cols(h)) for h in heads]
        log_betas, log_keeps = [], []
        for z in zs:
            soft = jnp.log2(1.0 + jnp.exp2(-jnp.abs(z)))
            log_beta = jnp.minimum(z, 0.0) - soft
            log_keep = log_beta - z
            if diag:
                log_keep = jnp.where(causal, log_keep, 0.0)
            log_betas.append(log_beta)
            log_keeps.append(log_keep)
        laters = [jnp.dot(tri, lk.astype(BF16), preferred_element_type=F32) for lk in log_keeps]
        ws = []
        for h in heads:
            arg = log_betas[h] + laters[h]
            if carries is not None:
                arg = arg + carries[h]
            w = jnp.exp2(arg)
            if diag:
                w = jnp.where(causal, w, 0.0)
            ws.append(w.astype(BF16))
        contribs = [jnp.dot(vt_ref[j, head_cols(h), :], ws[h], preferred_element_type=F32)
                    for h in heads]
        colsums = [jnp.sum(lk, axis=0, keepdims=True) for lk in log_keeps]
        return contribs, colsums

    def any_weight_left(carries):
        worst = functools.reduce(jnp.maximum, carries)
        return (jnp.max(worst) > EXP2_ZERO_BELOW).astype(jnp.int32)

    contribs, colsums = blocks(i, None, True)
    for h in heads:
        acc_ref[h] = contribs[h]
        carry_ref[h] = colsums[h]

    def cond(state):
        j, more = state
        return jnp.logical_and(j >= 0, more > 0)

    def body(state):
        j, _ = state
        carries = [carry_ref[h] for h in heads]
        contribs, colsums = blocks(j, carries, False)
        carries = [carries[h] + colsums[h] for h in heads]
        for h in heads:
            acc_ref[h] += contribs[h]
            carry_ref[h] = carries[h]
        return j - 1, any_weight_left(carries)

    lax.while_loop(cond, body, (i - 1, any_weight_left(colsums)))
    for h in range(SB_NH):
        o_ref[:, h * HEAD_DIM:(h + 1) * HEAD_DIM] = acc_ref[h].T.astype(o_ref.dtype)


def _sb_attention(proj3):
    width = SB_NH * HEAD_DIM
    groups = SB_HEADS // SB_NH
    return pl.pallas_call(
        _sb_attn_kernel,
        grid=(BATCH, groups, SEQ // ATT_TQ),
        in_specs=[pl.BlockSpec((None, ATT_TQ, width), lambda b, g, i: (b, i, g)),
                  pl.BlockSpec((None, SEQ, width), lambda b, g, i: (b, 0, groups + g)),
                  pl.BlockSpec((None, SEQ, width), lambda b, g, i: (b, 0, 2 * groups + g))],
        out_specs=pl.BlockSpec((None, ATT_TQ, width), lambda b, g, i: (b, i, g)),
        out_shape=jax.ShapeDtypeStruct((BATCH, SEQ, SB_WIDTH), BF16),
        scratch_shapes=[pltpu.VMEM((SEQ // ATT_BK, width, ATT_BK), BF16),
                        pltpu.VMEM((SB_NH, HEAD_DIM, ATT_TQ), F32),
                        pltpu.VMEM((SB_NH, 1, ATT_TQ), F32)],
        compiler_params=_params("arbitrary", "arbitrary", "arbitrary"),
        name="sb_attention",
    )(proj3, proj3, proj3)


def _diff_attn_kernel(lam_ref, g_ref, q_ref, k_ref, v_ref, o_ref, vt_ref, acc_ref, m_ref, l_ref,
                      z_ref, *, lambda_init):
    i = pl.program_id(1)

    @pl.when(i == 0)
    def _():
        _store_transposed_values(v_ref, vt_ref)

    key = lax.broadcasted_iota(jnp.int32, (ATT_BK, ATT_TQ), 0)
    qry = lax.broadcasted_iota(jnp.int32, (ATT_BK, ATT_TQ), 1)
    visible = _chunk_of(key) <= _chunk_of(qry)

    def head_cols(c):
        return slice(c * HEAD_DIM, (c + 1) * HEAD_DIM)

    def value_rows(c):
        return slice((c // 2) * DIFF_V_DIM, (c // 2 + 1) * DIFF_V_DIM)

    maps = range(DIFF_MAPS)

    def scores_all(j):
        return [_scores_t(k_ref, q_ref, j, head_cols(c)) for c in maps]

    def absorb(j):
        ps, alphas = [], []
        for c in maps:
            z = z_ref[c]
            m_old = m_ref[c]
            m_new = jnp.maximum(m_old, jnp.max(z, axis=0, keepdims=True))
            alpha = jnp.exp2(m_old - m_new)
            p = jnp.exp2(z - m_new)
            m_ref[c] = m_new
            l_ref[c] = l_ref[c] * alpha + jnp.sum(p, axis=0, keepdims=True)
            ps.append(p.astype(BF16))
            alphas.append(alpha)
        pvs = [jnp.dot(vt_ref[j, value_rows(c), :], ps[c], preferred_element_type=F32)
               for c in maps]
        for c in maps:
            acc_ref[c] = acc_ref[c] * alphas[c] + pvs[c]

    for c in maps:
        m_ref[c] = jnp.full((1, ATT_TQ), -jnp.inf, F32)
        l_ref[c] = jnp.zeros((1, ATT_TQ), F32)
        acc_ref[c] = jnp.zeros((DIFF_V_DIM, ATT_TQ), F32)
    for c, z in enumerate(scores_all(i)):
        z_ref[c] = jnp.where(visible, z, -jnp.inf)

    def body(jj, unused):
        j = i - jj
        zs_next = scores_all(j - 1)
        absorb(j)
        for c in maps:
            z_ref[c] = zs_next[c]
        return unused

    lax.fori_loop(0, i, body, 0)
    absorb(0)

    lam_rows = lam_ref[...]
    s1 = jnp.sum(lam_rows[0:1, :] * lam_rows[1:2, :], axis=-1, keepdims=True)
    s2 = jnp.sum(lam_rows[2:3, :] * lam_rows[3:4, :], axis=-1, keepdims=True)
    lam = jnp.exp(s1) - jnp.exp(s2) + lambda_init
    gain = g_ref[...]
    for h in range(DIFF_HEADS):
        o = acc_ref[2 * h] / l_ref[2 * h] - lam * (acc_ref[2 * h + 1] / l_ref[2 * h + 1])
        r = lax.rsqrt(jnp.mean(o * o, axis=0, keepdims=True) + EPS)
        o = o * r * gain * (1.0 - lambda_init)
        o_ref[:, h * DIFF_V_DIM:(h + 1) * DIFF_V_DIM] = o.T.astype(o_ref.dtype)


def _diff_attention(qk3, proj3, lam_rows, subln_col, lambda_init):
    v_tile = PLAIN_W_TILES.index(5)
    return pl.pallas_call(
        functools.partial(_diff_attn_kernel, lambda_init=lambda_init),
        grid=(BATCH, SEQ // ATT_TQ),
        in_specs=[pl.BlockSpec((4, HEAD_DIM), lambda b, i: (0, 0)),
                  pl.BlockSpec((DIFF_V_DIM, 1), lambda b, i: (0, 0)),
                  pl.BlockSpec((None, ATT_TQ, DIFF_QK_WIDTH), lambda b, i: (b, i, 0)),
                  pl.BlockSpec((None, SEQ, DIFF_QK_WIDTH), lambda b, i: (b, 0, 1)),
                  pl.BlockSpec((None, SEQ, DIFF_V_WIDTH), lambda b, i: (b, 0, v_tile))],
        out_specs=pl.BlockSpec((None, ATT_TQ, DIFF_V_WIDTH), lambda b, i: (b, i, 0)),
        out_shape=jax.ShapeDtypeStruct((BATCH, SEQ, DIFF_V_WIDTH), BF16),
        scratch_shapes=[pltpu.VMEM((SEQ // ATT_BK, DIFF_V_WIDTH, ATT_BK), BF16),
                        pltpu.VMEM((DIFF_MAPS, DIFF_V_DIM, ATT_TQ), F32),
                        pltpu.VMEM((DIFF_MAPS, 1, ATT_TQ), F32),
                        pltpu.VMEM((DIFF_MAPS, 1, ATT_TQ), F32),
                        pltpu.VMEM((DIFF_MAPS, ATT_BK, ATT_TQ), F32)],
        compiler_params=_params("arbitrary", "arbitrary"),
        name="diff_attention",
    )(lam_rows, subln_col, qk3, qk3, proj3)


def _out_proj_kernel(*refs, n_in):
    a_refs = refs[:n_in]
    w_refs = refs[n_in:2 * n_in]
    g_ref, h_ref, o_ref = refs[2 * n_in:]
    m = jnp.dot(a_refs[0][...], w_refs[0][...], preferred_element_type=F32)
    for a_ref, w_ref in zip(a_refs[1:], w_refs[1:]):
        m = m + jnp.dot(a_ref[...], w_ref[...], preferred_element_type=F32)
    o_ref[...] = m
    _add_normalized_rows(h_ref, o_ref, g_ref, o_ref)


def _out_proj(acts, w_stack, layer, g, h, name):
    n_in = len(acts)
    k = w_stack.shape[1] // n_in
    in_specs = ([pl.BlockSpec((OUT_TM, k), lambda i: (i, 0)) for _ in acts]
                + [pl.BlockSpec((None, k, D_MODEL), functools.partial(lambda i, r: (layer, r, 0), r=r))
                   for r in range(n_in)]
                + [pl.BlockSpec((1, D_MODEL), lambda i: (0, 0)),
                   pl.BlockSpec((OUT_TM, D_MODEL), lambda i: (i, 0))])
    return pl.pallas_call(
        functools.partial(_out_proj_kernel, n_in=n_in),
        grid=(TOKENS // OUT_TM,),
        in_specs=in_specs,
        out_specs=pl.BlockSpec((OUT_TM, D_MODEL), lambda i: (i, 0)),
        out_shape=jax.ShapeDtypeStruct((TOKENS, D_MODEL), F32),
        compiler_params=_params("arbitrary"),
        name=name,
    )(*acts, *([w_stack] * n_in), g, h)


def _ffn_kernel(h_ref, g1_ref, w1_ref, w2_ref, g2_ref, o_ref, hn_ref):
    f = pl.program_id(1)

    @pl.when(f == 0)
    def _():
        _normalize_rows(h_ref, g1_ref, hn_ref)
        o_ref[...] = jnp.zeros_like(o_ref)

    a = jnp.maximum(jnp.dot(hn_ref[...], w1_ref[...], preferred_element_type=F32), 0.0)
    o_ref[...] += jnp.dot((a * a).astype(BF16), w2_ref[...], preferred_element_type=F32)

    @pl.when(f == pl.num_programs(1) - 1)
    def _():
        _add_normalized_rows(h_ref, o_ref, g2_ref, o_ref)


def _ffn(h, g1, w1_stack, w2_stack, layer, g2):
    return pl.pallas_call(
        _ffn_kernel,
        grid=(TOKENS // FFN_TM, D_FF // FFN_TF),
        in_specs=[pl.BlockSpec((FFN_TM, D_MODEL), lambda i, f: (i, 0)),
                  pl.BlockSpec((1, D_MODEL), lambda i, f: (0, 0)),
                  pl.BlockSpec((None, D_MODEL, FFN_TF), lambda i, f: (layer, 0, f)),
                  pl.BlockSpec((None, FFN_TF, D_MODEL), lambda i, f: (layer, f, 0)),
                  pl.BlockSpec((1, D_MODEL), lambda i, f: (0, 0))],
        out_specs=pl.BlockSpec((FFN_TM, D_MODEL), lambda i, f: (i, 0)),
        out_shape=jax.ShapeDtypeStruct((TOKENS, D_MODEL), F32),
        scratch_shapes=[pltpu.VMEM((FFN_TM, D_MODEL), BF16)],
        compiler_params=_params("arbitrary", "arbitrary"),
        name="ffn",
    )(h, g1, w1_stack, w2_stack, g2)


def _ple_kernel(h_ref, p_ref, wg_ref, wp_ref, g_ref, o_ref):
    z = jnp.dot(h_ref[...].astype(BF16), wg_ref[...], preferred_element_type=F32)
    e = jnp.dot(p_ref[...].astype(BF16), wp_ref[...], preferred_element_type=F32)
    o_ref[...] = e / (1.0 + jnp.exp2(z * (-LOG2E)))
    _add_normalized_rows(h_ref, o_ref, g_ref, o_ref)


def _ple(h, p_stack, wg_stack, wp_stack, layer, g):
    return pl.pallas_call(
        _ple_kernel,
        grid=(TOKENS // PLE_TM,),
        in_specs=[pl.BlockSpec((PLE_TM, D_MODEL), lambda i: (i, 0)),
                  pl.BlockSpec((None, PLE_TM, PLE_DIM), lambda i: (layer, i, 0)),
                  pl.BlockSpec((None, D_MODEL, D_MODEL), lambda i: (layer, 0, 0)),
                  pl.BlockSpec((None, PLE_DIM, D_MODEL), lambda i: (layer, 0, 0)),
                  pl.BlockSpec((1, D_MODEL), lambda i: (0, 0))],
        out_specs=pl.BlockSpec((PLE_TM, D_MODEL), lambda i: (i, 0)),
        out_shape=jax.ShapeDtypeStruct((TOKENS, D_MODEL), F32),
        compiler_params=_params("arbitrary"),
        name="ple",
    )(h, p_stack, wg_stack, wp_stack, g)


def _sg_kernel(u_ref, v_ref, lng_ref, lnb_ref, ws_ref, bst_ref, y_ref, vn_ref):
    v = v_ref[...].astype(F32)
    mu = jnp.mean(v, axis=-1, keepdims=True)
    vc = v - mu
    inv = lax.rsqrt(jnp.mean(vc * vc, axis=-1, keepdims=True) + EPS)
    vn_ref[...] = (vc * inv * lng_ref[...] + lnb_ref[...]).astype(BF16)

    n_blk = SG_TM // SG_BLOCK
    t_idx = lax.broadcasted_iota(jnp.int32, (SG_BLOCK, SG_BLOCK), 0)
    s_idx = lax.broadcasted_iota(jnp.int32, (SG_BLOCK, SG_BLOCK), 1)
    visible = _chunk_of(s_idx) <= _chunk_of(t_idx)
    bst = bst_ref[...]
    for g in range(SG_GROUPS):
        cols = slice(g * SG_GROUP_DIM, (g + 1) * SG_GROUP_DIM)
        w = jnp.where(visible, ws_ref[g], 0.0).astype(BF16)
        rhs = jnp.concatenate(
            [vn_ref[n * SG_BLOCK:(n + 1) * SG_BLOCK, cols] for n in range(n_blk)], axis=1)
        mixed = jnp.dot(w, rhs, preferred_element_type=F32) + bst[:, g:g + 1]
        for n in range(n_blk):
            rows = slice(n * SG_BLOCK, (n + 1) * SG_BLOCK)
            u = u_ref[rows, cols].astype(F32)
            y_ref[rows, cols] = (u * mixed[:, n * SG_BLOCK:(n + 1) * SG_BLOCK]).astype(y_ref.dtype)


def _spatial_gate(uv, ln_g, ln_b, w_s_stack, layer, b_s_t):
    return pl.pallas_call(
        _sg_kernel,
        grid=(TOKENS // SG_TM,),
        in_specs=[pl.BlockSpec((SG_TM, SG_WIDTH), lambda i: (i, 0)),
                  pl.BlockSpec((SG_TM, SG_WIDTH), lambda i: (i, 1)),
                  pl.BlockSpec((1, SG_WIDTH), lambda i: (0, 0)),
                  pl.BlockSpec((1, SG_WIDTH), lambda i: (0, 0)),
                  pl.BlockSpec((None, SG_GROUPS, SG_BLOCK, SG_BLOCK), lambda i: (layer, 0, 0, 0)),
                  pl.BlockSpec((SG_BLOCK, SG_GROUPS), lambda i: (0, 0))],
        out_specs=pl.BlockSpec((SG_TM, SG_WIDTH), lambda i: (i, 0)),
        out_shape=jax.ShapeDtypeStruct((TOKENS, SG_WIDTH), BF16),
        scratch_shapes=[pltpu.VMEM((SG_TM, SG_WIDTH), BF16)],
        compiler_params=_params("arbitrary"),
        name="spatial_gate",
    )(uv, uv, ln_g, ln_b, w_s_stack, b_s_t)


def _row(v):
    return v.reshape(1, -1)


def kernel(x, p, positions, ev_norm_pre, ev_w_in, ev_lam_q1, ev_lam_k1, ev_lam_q2, ev_lam_k2,
           ev_subln, ev_w_out, ev_norm_post, od_norm_pre, od_w_in, od_ln_g, od_ln_b, od_w_s,
           od_b_s, od_w_out, od_norm_post, ffn_norm_pre, ffn_w1, ffn_w2, ffn_norm_post,
           ple_w_proj, ple_w_gate, ple_norm):
    h = x.reshape(TOKENS, D_MODEL)
    p_stack = p.reshape(DEPTH, TOKENS, PLE_DIM)
    cos_t, sin_t = _rope_tables(positions)
    ev_w_in, ev_w_out, od_w_in, od_w_out, ffn_w1, ffn_w2, ple_w_proj, ple_w_gate = (
        w.astype(BF16) for w in
        (ev_w_in, ev_w_out, od_w_in, od_w_out, ffn_w1, ffn_w2, ple_w_proj, ple_w_gate))
    sb_q_scale = jnp.concatenate([jnp.full((1, SB_WIDTH), Q_SCALE, F32),
                                  jnp.ones((1, (len(PLAIN_W_TILES) - 1) * PROJ_TN), F32)], axis=1)

    for i in range(DEPTH):
        j = i // 2
        if i % 2 == 0:
            lambda_init = 0.8 - 0.6 * math.exp(-0.3 * i)
            proj, xn = _plain_proj(h, _row(ev_norm_pre[j]), ev_w_in, j, sb_q_scale)
            qk = _rope_proj(xn, ev_w_in, j, cos_t, sin_t)
            proj3 = proj.reshape(BATCH, SEQ, len(PLAIN_W_TILES) * PROJ_TN)
            qk3 = qk.reshape(BATCH, SEQ, 2 * DIFF_QK_WIDTH)
            sb_o = _sb_attention(proj3).reshape(TOKENS, SB_WIDTH)
            lam_rows = jnp.stack([ev_lam_q1[j], ev_lam_k1[j], ev_lam_q2[j], ev_lam_k2[j]])
            df_o = _diff_attention(qk3, proj3, lam_rows, ev_subln[j].reshape(DIFF_V_DIM, 1),
                                   lambda_init).reshape(TOKENS, DIFF_V_WIDTH)
            h = _out_proj([sb_o, df_o], ev_w_out, j, _row(ev_norm_post[j]), h, "even_out_proj")
        else:
            uv = _odd_proj(h, _row(od_norm_pre[j]), od_w_in, j)
            y = _spatial_gate(uv, _row(od_ln_g[j]), _row(od_ln_b[j]), od_w_s, j, od_b_s[j].T)
            h = _out_proj([y], od_w_out, j, _row(od_norm_post[j]), h, "odd_out_proj")
        h = _ffn(h, _row(ffn_norm_pre[i]), ffn_w1, ffn_w2, i, _row(ffn_norm_post[i]))
        h = _ple(h, p_stack, ple_w_gate, ple_w_proj, i, _row(ple_norm[i]))
    return h.reshape(BATCH, SEQ, D_MODEL)
```

```python
import functools
import math
from typing import NamedTuple

import jax
import jax.numpy as jnp
from jax import lax
from jax.experimental import pallas as pl
from jax.experimental.pallas import tpu as pltpu

D_MODEL = 2048
BATCH = 8
SEQ = 2048
DEPTH = 2
TOKENS = BATCH * SEQ

CHUNK = 64
HEAD_DIM = 128
SB_HEADS = 8
DIFF_HEADS = 4
DIFF_MAPS = 2 * DIFF_HEADS
DIFF_V_DIM = 2 * HEAD_DIM
ROT_DIM = HEAD_DIM // 4
ROT_HALF = ROT_DIM // 2
ROPE_THETA = 500000.0
SG_BLOCK = 128
SG_GROUPS = 16
SG_GROUP_DIM = 128
SG_WIDTH = SG_GROUPS * SG_GROUP_DIM
D_FF = 4 * D_MODEL
PLE_DIM = 256
SB_WIDTH = SB_HEADS * HEAD_DIM
DIFF_QK_WIDTH = DIFF_HEADS * 2 * HEAD_DIM
DIFF_V_WIDTH = DIFF_HEADS * DIFF_V_DIM
EVEN_IN_WIDTH = 3 * SB_WIDTH + 2 * DIFF_QK_WIDTH + DIFF_V_WIDTH
EPS = 1e-6
LOG2E = math.log2(math.e)
Q_SCALE = HEAD_DIM ** -0.5 * LOG2E
EXP2_ZERO_BELOW = -151.0

LANES = 128
BF16_SUBLANES = 16
VMEM_LIMIT_BYTES = 56 * 1024 * 1024

F32 = jnp.float32
BF16 = jnp.bfloat16

PROJ_TM = 1024
PROJ_TN = 1024
ATT_TQ = 256
ATT_BK = 256
SB_NH = 8
OUT_TM = 512
FFN_TM = 512
FFN_TF = 1024
PLE_TM = 512
SG_TM = 512
ROPE_TM = 2048
NORM_ROWS = 16

PLAIN_W_TILES = (0, 1, 2, 5)
ROPE_W_TILE0 = 3


def _chunk_of(idx):
    return jnp.right_shift(idx, CHUNK.bit_length() - 1)


def _rms_scale(x):
    return lax.rsqrt(jnp.mean(x * x, axis=-1, keepdims=True) + EPS)


class _Cast(NamedTuple):
    stack: jax.Array
    layer: int


def _no_rider():
    pass


def _call(body, *, name, grid, in_specs, out_specs, out_shape, args, scratch_shapes=(), casts=()):
    steps = math.prod(grid)
    n_in, n_out, n_cast = len(in_specs), len(out_specs), len(casts)

    def linear_step(*idx):
        step = idx[0]
        for extent, k in zip(grid[1:], idx[1:]):
            step = step * extent + k
        return step

    cast_in, cast_out, cast_shape = [], [], []
    for job in casts:
        _, rows, cols = job.stack.shape
        assert rows % BF16_SUBLANES == 0
        n_blocks = math.gcd(steps, rows // BF16_SUBLANES)
        blk_rows = rows // n_blocks
        repeat = steps // n_blocks
        cast_in.append(pl.BlockSpec(
            (None, blk_rows, cols),
            functools.partial(lambda *idx, layer, repeat: (layer, linear_step(*idx) // repeat, 0),
                              layer=job.layer, repeat=repeat)))
        cast_out.append(pl.BlockSpec(
            (blk_rows, cols),
            functools.partial(lambda *idx, repeat: (linear_step(*idx) // repeat, 0), repeat=repeat)))
        cast_shape.append(jax.ShapeDtypeStruct((rows, cols), BF16))

    def wrapped(*refs):
        ins = refs[:n_in]
        srcs = refs[n_in:n_in + n_cast]
        outs = refs[n_in + n_cast:n_in + n_cast + n_out]
        dsts = refs[n_in + n_cast + n_out:n_in + 2 * n_cast + n_out]
        scratch = refs[n_in + 2 * n_cast + n_out:]

        def rider():
            for src, dst in zip(srcs, dsts):
                dst[...] = src[...].astype(dst.dtype)

        body(*ins, *outs, *scratch, rider=rider if n_cast else _no_rider)

    results = pl.pallas_call(
        wrapped,
        grid=grid,
        in_specs=list(in_specs) + cast_in,
        out_specs=list(out_specs) + cast_out,
        out_shape=list(out_shape) + cast_shape,
        scratch_shapes=list(scratch_shapes),
        compiler_params=pltpu.CompilerParams(dimension_semantics=("arbitrary",) * len(grid),
                                             vmem_limit_bytes=VMEM_LIMIT_BYTES),
        name=name,
    )(*args, *[job.stack for job in casts])
    return results[:n_out], results[n_out:]


def _rope_table_kernel(pos_ref, freq_ref, cos_ref, sin_ref, *, rider):
    rider()
    ang = pos_ref[...].astype(F32) * freq_ref[...]
    lane = lax.broadcasted_iota(jnp.int32, ang.shape, 1)
    c = jnp.where(lane < ROT_DIM, jnp.cos(ang), 1.0)
    s = jnp.sin(ang)
    s = jnp.where(lane < ROT_HALF, -s, jnp.where(lane < ROT_DIM, s, 0.0))
    cos_ref[0] = c * Q_SCALE
    sin_ref[0] = s * Q_SCALE
    cos_ref[1] = c
    sin_ref[1] = s


def _rope_tables(positions):
    inv_freq = ROPE_THETA ** (-jnp.arange(0, ROT_DIM, 2, dtype=F32) / ROT_DIM)
    freq_row = jnp.concatenate(
        [inv_freq, inv_freq, jnp.zeros((LANES - ROT_DIM,), F32)]).reshape(1, LANES)
    pos = jnp.broadcast_to(positions.reshape(TOKENS, 1), (TOKENS, LANES))
    (cos_t, sin_t), _ = _call(
        _rope_table_kernel,
        name="rope_tables",
        grid=(TOKENS // ROPE_TM,),
        in_specs=[pl.BlockSpec((ROPE_TM, LANES), lambda i: (i, 0)),
                  pl.BlockSpec((1, LANES), lambda i: (0, 0))],
        out_specs=[pl.BlockSpec((2, ROPE_TM, LANES), lambda i: (0, i, 0)),
                   pl.BlockSpec((2, ROPE_TM, LANES), lambda i: (0, i, 0))],
        out_shape=[jax.ShapeDtypeStruct((2, TOKENS, LANES), F32)] * 2,
        args=(pos, freq_row))
    return cos_t, sin_t


def _normalize_rows(x_ref, g_ref, xn_ref):
    g = g_ref[...]
    for r0 in range(0, x_ref.shape[0], NORM_ROWS):
        x = x_ref[r0:r0 + NORM_ROWS, :]
        xn_ref[r0:r0 + NORM_ROWS, :] = (x * _rms_scale(x) * g).astype(BF16)


def _add_normalized_rows(h_ref, m_ref, g_ref, o_ref):
    g = g_ref[...]
    for r0 in range(0, h_ref.shape[0], NORM_ROWS):
        rows = slice(r0, r0 + NORM_ROWS)
        m = m_ref[rows, :]
        o_ref[rows, :] = h_ref[rows, :] + m * _rms_scale(m) * g


def _plain_proj_kernel(x_ref, g_ref, w_ref, cs_ref, o_ref, xn_ref, *, rider):
    @pl.when(pl.program_id(1) == 0)
    def _():
        _normalize_rows(x_ref, g_ref, xn_ref)

    rider()
    acc = jnp.dot(xn_ref[...], w_ref[...], preferred_element_type=F32)
    o_ref[...] = (acc * cs_ref[...]).astype(o_ref.dtype)


def _plain_proj(h, g, w, col_scale, casts):
    n_tiles = len(PLAIN_W_TILES)
    first_gap = PLAIN_W_TILES.index(5)

    def w_map(i, j):
        return (0, jnp.where(j < first_gap, j, j + (5 - first_gap)))

    return _call(
        _plain_proj_kernel,
        name="even_plain_proj",
        grid=(TOKENS // PROJ_TM, n_tiles),
        in_specs=[pl.BlockSpec((PROJ_TM, D_MODEL), lambda i, j: (i, 0)),
                  pl.BlockSpec((1, D_MODEL), lambda i, j: (0, 0)),
                  pl.BlockSpec((D_MODEL, PROJ_TN), w_map),
                  pl.BlockSpec((1, PROJ_TN), lambda i, j: (0, j))],
        out_specs=[pl.BlockSpec((PROJ_TM, PROJ_TN), lambda i, j: (i, j)),
                   pl.BlockSpec((PROJ_TM, D_MODEL), lambda i, j: (i, 0))],
        out_shape=[jax.ShapeDtypeStruct((TOKENS, n_tiles * PROJ_TN), BF16),
                   jax.ShapeDtypeStruct((TOKENS, D_MODEL), BF16)],
        args=(h, g, w, col_scale),
        casts=casts)


def _rope_proj_kernel(xn_ref, w_ref, cos_ref, sin_ref, o_ref, *, rider):
    rider()
    acc = jnp.dot(xn_ref[...], w_ref[...], preferred_element_type=F32)
    c = cos_ref[...]
    s = sin_ref[...]
    first = lax.broadcasted_iota(jnp.int32, c.shape, 1) < ROT_HALF
    for g in range(PROJ_TN // LANES):
        a = acc[:, g * LANES:(g + 1) * LANES]
        swapped = jnp.where(first, pltpu.roll(a, LANES - ROT_HALF, 1), pltpu.roll(a, ROT_HALF, 1))
        o_ref[:, g * LANES:(g + 1) * LANES] = (a * c + swapped * s).astype(o_ref.dtype)


def _rope_proj(xn, w, cos_t, sin_t, casts):
    return _call(
        _rope_proj_kernel,
        name="even_rope_proj",
        grid=(TOKENS // PROJ_TM, 2),
        in_specs=[pl.BlockSpec((PROJ_TM, D_MODEL), lambda i, j: (i, 0)),
                  pl.BlockSpec((D_MODEL, PROJ_TN), lambda i, j: (0, ROPE_W_TILE0 + j)),
                  pl.BlockSpec((None, PROJ_TM, LANES), lambda i, j: (j, i, 0)),
                  pl.BlockSpec((None, PROJ_TM, LANES), lambda i, j: (j, i, 0))],
        out_specs=[pl.BlockSpec((PROJ_TM, PROJ_TN), lambda i, j: (i, j))],
        out_shape=[jax.ShapeDtypeStruct((TOKENS, 2 * PROJ_TN), BF16)],
        args=(xn, w, cos_t, sin_t),
        casts=casts)


def _gelu_tanh(x):
    c = -2.0 * math.sqrt(2.0 / math.pi) * LOG2E
    return x / (1.0 + jnp.exp2(x * (c + (0.044715 * c) * (x * x))))


def _odd_proj_kernel(x_ref, g_ref, w_ref, o_ref, xn_ref, *, rider):
    @pl.when(pl.program_id(1) == 0)
    def _():
        _normalize_rows(x_ref, g_ref, xn_ref)

    rider()
    acc = jnp.dot(xn_ref[...], w_ref[...], preferred_element_type=F32)
    o_ref[...] = _gelu_tanh(acc).astype(o_ref.dtype)


def _odd_proj(h, g, w):
    n = w.shape[1]
    (uv,), _ = _call(
        _odd_proj_kernel,
        name="odd_in_proj",
        grid=(TOKENS // PROJ_TM, n // PROJ_TN),
        in_specs=[pl.BlockSpec((PROJ_TM, D_MODEL), lambda i, j: (i, 0)),
                  pl.BlockSpec((1, D_MODEL), lambda i, j: (0, 0)),
                  pl.BlockSpec((D_MODEL, PROJ_TN), lambda i, j: (0, j))],
        out_specs=[pl.BlockSpec((PROJ_TM, PROJ_TN), lambda i, j: (i, j))],
        out_shape=[jax.ShapeDtypeStruct((TOKENS, n), BF16)],
        scratch_shapes=[pltpu.VMEM((PROJ_TM, D_MODEL), BF16)],
        args=(h, g, w))
    return uv


def _store_transposed_values(v_ref, vt_ref):
    for jb in range(SEQ // ATT_BK):
        vt_ref[jb] = v_ref[jb * ATT_BK:(jb + 1) * ATT_BK, :].astype(F32).T.astype(BF16)


def _scores_t(k_ref, q_ref, j, cols):
    kj = k_ref[pl.ds(j * ATT_BK, ATT_BK), cols]
    return lax.dot_general(kj, q_ref[:, cols], (((1,), (1,)), ((), ())),
                           preferred_element_type=F32)


def _sb_attn_kernel(q_ref, k_ref, v_ref, o_ref, vt_ref, acc_ref, carry_ref, *, rider):
    i = pl.program_id(2)

    @pl.when(i == 0)
    def _():
        _store_transposed_values(v_ref, vt_ref)

    rider()
    key = lax.broadcasted_iota(jnp.int32, (ATT_BK, ATT_TQ), 0)
    qry = lax.broadcasted_iota(jnp.int32, (ATT_BK, ATT_TQ), 1)
    causal = key < qry
    later_key = lax.broadcasted_iota(jnp.int32, (ATT_BK, ATT_BK), 1)
    this_key = lax.broadcasted_iota(jnp.int32, (ATT_BK, ATT_BK), 0)
    tri = (later_key > this_key).astype(BF16)

    heads = range(SB_NH)

    def head_cols(h):
        return slice(h * HEAD_DIM, (h + 1) * HEAD_DIM)

    def blocks(j, carries, diag):
        zs = [_scores_t(k_ref, q_ref, j, head_cols(h)) for h in heads]
        log_betas, log_keeps = [], []
        for z in zs:
            soft = jnp.log2(1.0 + jnp.exp2(-jnp.abs(z)))
            log_beta = jnp.minimum(z, 0.0) - soft
            log_keep = log_beta - z
            if diag:
                log_keep = jnp.where(causal, log_keep, 0.0)
            log_betas.append(log_beta)
            log_keeps.append(log_keep)
        laters = [jnp.dot(tri, lk.astype(BF16), preferred_element_type=F32) for lk in log_keeps]
        ws = []
        for h in heads:
            arg = log_betas[h] + laters[h]
            if carries is not None:
                arg = arg + carries[h]
            w = jnp.exp2(arg)
            if diag:
                w = jnp.where(causal, w, 0.0)
            ws.append(w.astype(BF16))
        contribs = [jnp.dot(vt_ref[j, head_cols(h), :], ws[h], preferred_element_type=F32)
                    for h in heads]
        colsums = [jnp.sum(lk, axis=0, keepdims=True) for lk in log_keeps]
        return contribs, colsums

    def any_weight_left(carries):
        worst = functools.reduce(jnp.maximum, carries)
        return (jnp.max(worst) > EXP2_ZERO_BELOW).astype(jnp.int32)

    contribs, colsums = blocks(i, None, True)
    for h in heads:
        acc_ref[h] = contribs[h]
        carry_ref[h] = colsums[h]

    def cond(state):
        j, more = state
        return jnp.logical_and(j >= 0, more > 0)

    def body(state):
        j, _ = state
        carries = [carry_ref[h] for h in heads]
        contribs, colsums = blocks(j, carries, False)
        carries = [carries[h] + colsums[h] for h in heads]
        for h in heads:
            acc_ref[h] += contribs[h]
            carry_ref[h] = carries[h]
        return j - 1, any_weight_left(carries)

    lax.while_loop(cond, body, (i - 1, any_weight_left(colsums)))
    for h in range(SB_NH):
        o_ref[:, h * HEAD_DIM:(h + 1) * HEAD_DIM] = acc_ref[h].T.astype(o_ref.dtype)


def _sb_attention(proj3, casts):
    width = SB_NH * HEAD_DIM
    groups = SB_HEADS // SB_NH
    return _call(
        _sb_attn_kernel,
        name="sb_attention",
        grid=(BATCH, groups, SEQ // ATT_TQ),
        in_specs=[pl.BlockSpec((None, ATT_TQ, width), lambda b, g, i: (b, i, g)),
                  pl.BlockSpec((None, SEQ, width), lambda b, g, i: (b, 0, groups + g)),
                  pl.BlockSpec((None, SEQ, width), lambda b, g, i: (b, 0, 2 * groups + g))],
        out_specs=[pl.BlockSpec((None, ATT_TQ, width), lambda b, g, i: (b, i, g))],
        out_shape=[jax.ShapeDtypeStruct((BATCH, SEQ, SB_WIDTH), BF16)],
        scratch_shapes=[pltpu.VMEM((SEQ // ATT_BK, width, ATT_BK), BF16),
                        pltpu.VMEM((SB_NH, HEAD_DIM, ATT_TQ), F32),
                        pltpu.VMEM((SB_NH, 1, ATT_TQ), F32)],
        args=(proj3, proj3, proj3),
        casts=casts)


def _diff_attn_kernel(lam_ref, g_ref, q_ref, k_ref, v_ref, o_ref, vt_ref, acc_ref, m_ref, l_ref,
                      z_ref, *, lambda_init, rider):
    i = pl.program_id(1)

    @pl.when(i == 0)
    def _():
        _store_transposed_values(v_ref, vt_ref)

    rider()
    key = lax.broadcasted_iota(jnp.int32, (ATT_BK, ATT_TQ), 0)
    qry = lax.broadcasted_iota(jnp.int32, (ATT_BK, ATT_TQ), 1)
    visible = _chunk_of(key) <= _chunk_of(qry)

    def head_cols(c):
        return slice(c * HEAD_DIM, (c + 1) * HEAD_DIM)

    def value_rows(c):
        return slice((c // 2) * DIFF_V_DIM, (c // 2 + 1) * DIFF_V_DIM)

    maps = range(DIFF_MAPS)

    def scores_all(j):
        return [_scores_t(k_ref, q_ref, j, head_cols(c)) for c in maps]

    def absorb(j):
        ps, alphas = [], []
        for c in maps:
            z = z_ref[c]
            m_old = m_ref[c]
            m_new = jnp.maximum(m_old, jnp.max(z, axis=0, keepdims=True))
            alpha = jnp.exp2(m_old - m_new)
            p = jnp.exp2(z - m_new)
            m_ref[c] = m_new
            l_ref[c] = l_ref[c] * alpha + jnp.sum(p, axis=0, keepdims=True)
            ps.append(p.astype(BF16))
            alphas.append(alpha)
        pvs = [jnp.dot(vt_ref[j, value_rows(c), :], ps[c], preferred_element_type=F32)
               for c in maps]
        for c in maps:
            acc_ref[c] = acc_ref[c] * alphas[c] + pvs[c]

    for c in maps:
        m_ref[c] = jnp.full((1, ATT_TQ), -jnp.inf, F32)
        l_ref[c] = jnp.zeros((1, ATT_TQ), F32)
        acc_ref[c] = jnp.zeros((DIFF_V_DIM, ATT_TQ), F32)
    for c, z in enumerate(scores_all(i)):
        z_ref[c] = jnp.where(visible, z, -jnp.inf)

    def body(jj, unused):
        j = i - jj
        zs_next = scores_all(j - 1)
        absorb(j)
        for c in maps:
            z_ref[c] = zs_next[c]
        return unused

    lax.fori_loop(0, i, body, 0)
    absorb(0)

    lam_rows = lam_ref[...]
    s1 = jnp.sum(lam_rows[0:1, :] * lam_rows[1:2, :], axis=-1, keepdims=True)
    s2 = jnp.sum(lam_rows[2:3, :] * lam_rows[3:4, :], axis=-1, keepdims=True)
    lam = jnp.exp(s1) - jnp.exp(s2) + lambda_init
    gain = g_ref[...]
    for h in range(DIFF_HEADS):
        o = acc_ref[2 * h] / l_ref[2 * h] - lam * (acc_ref[2 * h + 1] / l_ref[2 * h + 1])
        r = lax.rsqrt(jnp.mean(o * o, axis=0, keepdims=True) + EPS)
        o = o * r * gain * (1.0 - lambda_init)
        o_ref[:, h * DIFF_V_DIM:(h + 1) * DIFF_V_DIM] = o.T.astype(o_ref.dtype)


def _diff_attention(qk3, proj3, lam_rows, subln_col, lambda_init, casts):
    v_tile = PLAIN_W_TILES.index(5)
    return _call(
        functools.partial(_diff_attn_kernel, lambda_init=lambda_init),
        name="diff_attention",
        grid=(BATCH, SEQ // ATT_TQ),
        in_specs=[pl.BlockSpec((4, HEAD_DIM), lambda b, i: (0, 0)),
                  pl.BlockSpec((DIFF_V_DIM, 1), lambda b, i: (0, 0)),
                  pl.BlockSpec((None, ATT_TQ, DIFF_QK_WIDTH), lambda b, i: (b, i, 0)),
                  pl.BlockSpec((None, SEQ, DIFF_QK_WIDTH), lambda b, i: (b, 0, 1)),
                  pl.BlockSpec((None, SEQ, DIFF_V_WIDTH), lambda b, i: (b, 0, v_tile))],
        out_specs=[pl.BlockSpec((None, ATT_TQ, DIFF_V_WIDTH), lambda b, i: (b, i, 0))],
        out_shape=[jax.ShapeDtypeStruct((BATCH, SEQ, DIFF_V_WIDTH), BF16)],
        scratch_shapes=[pltpu.VMEM((SEQ // ATT_BK, DIFF_V_WIDTH, ATT_BK), BF16),
                        pltpu.VMEM((DIFF_MAPS, DIFF_V_DIM, ATT_TQ), F32),
                        pltpu.VMEM((DIFF_MAPS, 1, ATT_TQ), F32),
                        pltpu.VMEM((DIFF_MAPS, 1, ATT_TQ), F32),
                        pltpu.VMEM((DIFF_MAPS, ATT_BK, ATT_TQ), F32)],
        args=(lam_rows, subln_col, qk3, qk3, proj3),
        casts=casts)


def _out_proj_kernel(*refs, n_in, rider):
    a_refs = refs[:n_in]
    w_refs = refs[n_in:2 * n_in]
    g_ref, h_ref, o_ref = refs[2 * n_in:]
    rider()
    m = jnp.dot(a_refs[0][...], w_refs[0][...], preferred_element_type=F32)
    for a_ref, w_ref in zip(a_refs[1:], w_refs[1:]):
        m = m + jnp.dot(a_ref[...], w_ref[...], preferred_element_type=F32)
    o_ref[...] = m
    _add_normalized_rows(h_ref, o_ref, g_ref, o_ref)


def _out_proj(acts, w, g, h, name):
    n_in = len(acts)
    k = w.shape[0] // n_in
    in_specs = ([pl.BlockSpec((OUT_TM, k), lambda i: (i, 0)) for _ in acts]
                + [pl.BlockSpec((k, D_MODEL), functools.partial(lambda i, r: (r, 0), r=r))
                   for r in range(n_in)]
                + [pl.BlockSpec((1, D_MODEL), lambda i: (0, 0)),
                   pl.BlockSpec((OUT_TM, D_MODEL), lambda i: (i, 0))])
    (h_out,), _ = _call(
        functools.partial(_out_proj_kernel, n_in=n_in),
        name=name,
        grid=(TOKENS // OUT_TM,),
        in_specs=in_specs,
        out_specs=[pl.BlockSpec((OUT_TM, D_MODEL), lambda i: (i, 0))],
        out_shape=[jax.ShapeDtypeStruct((TOKENS, D_MODEL), F32)],
        args=(*acts, *([w] * n_in), g, h))
    return h_out


def _ffn_kernel(h_ref, g1_ref, w1_ref, w2_ref, g2_ref, o_ref, hn_ref, *, rider):
    f = pl.program_id(1)

    @pl.when(f == 0)
    def _():
        _normalize_rows(h_ref, g1_ref, hn_ref)
        o_ref[...] = jnp.zeros_like(o_ref)

    rider()
    a = jnp.maximum(jnp.dot(hn_ref[...], w1_ref[...], preferred_element_type=F32), 0.0)
    o_ref[...] += jnp.dot((a * a).astype(BF16), w2_ref[...], preferred_element_type=F32)

    @pl.when(f == pl.num_programs(1) - 1)
    def _():
        _add_normalized_rows(h_ref, o_ref, g2_ref, o_ref)


def _ffn(h, g1, w1, w2, g2, casts):
    return _call(
        _ffn_kernel,
        name="ffn",
        grid=(TOKENS // FFN_TM, D_FF // FFN_TF),
        in_specs=[pl.BlockSpec((FFN_TM, D_MODEL), lambda i, f: (i, 0)),
                  pl.BlockSpec((1, D_MODEL), lambda i, f: (0, 0)),
                  pl.BlockSpec((D_MODEL, FFN_TF), lambda i, f: (0, f)),
                  pl.BlockSpec((FFN_TF, D_MODEL), lambda i, f: (f, 0)),
                  pl.BlockSpec((1, D_MODEL), lambda i, f: (0, 0))],
        out_specs=[pl.BlockSpec((FFN_TM, D_MODEL), lambda i, f: (i, 0))],
        out_shape=[jax.ShapeDtypeStruct((TOKENS, D_MODEL), F32)],
        scratch_shapes=[pltpu.VMEM((FFN_TM, D_MODEL), BF16)],
        args=(h, g1, w1, w2, g2),
        casts=casts)


def _ple_kernel(h_ref, p_ref, wg_ref, wp_ref, g_ref, o_ref, *, rider):
    rider()
    z = jnp.dot(h_ref[...].astype(BF16), wg_ref[...], preferred_element_type=F32)
    e = jnp.dot(p_ref[...].astype(BF16), wp_ref[...], preferred_element_type=F32)
    o_ref[...] = e / (1.0 + jnp.exp2(z * (-LOG2E)))
    _add_normalized_rows(h_ref, o_ref, g_ref, o_ref)


def _ple(h, p_stack, layer, wg, wp, g):
    (h_out,), _ = _call(
        _ple_kernel,
        name="ple",
        grid=(TOKENS // PLE_TM,),
        in_specs=[pl.BlockSpec((PLE_TM, D_MODEL), lambda i: (i, 0)),
                  pl.BlockSpec((None, PLE_TM, PLE_DIM), lambda i: (layer, i, 0)),
                  pl.BlockSpec((D_MODEL, D_MODEL), lambda i: (0, 0)),
                  pl.BlockSpec((PLE_DIM, D_MODEL), lambda i: (0, 0)),
                  pl.BlockSpec((1, D_MODEL), lambda i: (0, 0))],
        out_specs=[pl.BlockSpec((PLE_TM, D_MODEL), lambda i: (i, 0))],
        out_shape=[jax.ShapeDtypeStruct((TOKENS, D_MODEL), F32)],
        args=(h, p_stack, wg, wp, g))
    return h_out


def _sg_kernel(u_ref, v_ref, lng_ref, lnb_ref, ws_ref, bst_ref, y_ref, vn_ref, *, rider):
    rider()
    v = v_ref[...].astype(F32)
    mu = jnp.mean(v, axis=-1, keepdims=True)
    vc = v - mu
    inv = lax.rsqrt(jnp.mean(vc * vc, axis=-1, keepdims=True) + EPS)
    vn_ref[...] = (vc * inv * lng_ref[...] + lnb_ref[...]).astype(BF16)

    n_blk = SG_TM // SG_BLOCK
    t_idx = lax.broadcasted_iota(jnp.int32, (SG_BLOCK, SG_BLOCK), 0)
    s_idx = lax.broadcasted_iota(jnp.int32, (SG_BLOCK, SG_BLOCK), 1)
    visible = _chunk_of(s_idx) <= _chunk_of(t_idx)
    bst = bst_ref[...]
    for g in range(SG_GROUPS):
        cols = slice(g * SG_GROUP_DIM, (g + 1) * SG_GROUP_DIM)
        w = jnp.where(visible, ws_ref[g], 0.0).astype(BF16)
        rhs = jnp.concatenate(
            [vn_ref[n * SG_BLOCK:(n + 1) * SG_BLOCK, cols] for n in range(n_blk)], axis=1)
        mixed = jnp.dot(w, rhs, preferred_element_type=F32) + bst[:, g:g + 1]
        for n in range(n_blk):
            rows = slice(n * SG_BLOCK, (n + 1) * SG_BLOCK)
            u = u_ref[rows, cols].astype(F32)
            y_ref[rows, cols] = (u * mixed[:, n * SG_BLOCK:(n + 1) * SG_BLOCK]).astype(y_ref.dtype)


def _spatial_gate(uv, ln_g, ln_b, w_s_stack, layer, b_s_t):
    (y,), _ = _call(
        _sg_kernel,
        name="spatial_gate",
        grid=(TOKENS // SG_TM,),
        in_specs=[pl.BlockSpec((SG_TM, SG_WIDTH), lambda i: (i, 0)),
                  pl.BlockSpec((SG_TM, SG_WIDTH), lambda i: (i, 1)),
                  pl.BlockSpec((1, SG_WIDTH), lambda i: (0, 0)),
                  pl.BlockSpec((1, SG_WIDTH), lambda i: (0, 0)),
                  pl.BlockSpec((None, SG_GROUPS, SG_BLOCK, SG_BLOCK), lambda i: (layer, 0, 0, 0)),
                  pl.BlockSpec((SG_BLOCK, SG_GROUPS), lambda i: (0, 0))],
        out_specs=[pl.BlockSpec((SG_TM, SG_WIDTH), lambda i: (i, 0))],
        out_shape=[jax.ShapeDtypeStruct((TOKENS, SG_WIDTH), BF16)],
        scratch_shapes=[pltpu.VMEM((SG_TM, SG_WIDTH), BF16)],
        args=(uv, uv, ln_g, ln_b, w_s_stack, b_s_t))
    return y


def _row(v):
    return v.reshape(1, -1)


def kernel(x, p, positions, ev_norm_pre, ev_w_in, ev_lam_q1, ev_lam_k1, ev_lam_q2, ev_lam_k2,
           ev_subln, ev_w_out, ev_norm_post, od_norm_pre, od_w_in, od_ln_g, od_ln_b, od_w_s,
           od_b_s, od_w_out, od_norm_post, ffn_norm_pre, ffn_w1, ffn_w2, ffn_norm_post,
           ple_w_proj, ple_w_gate, ple_norm):
    assert DEPTH == 2, "the cast schedule below is written for one even and one odd layer"
    h = x.reshape(TOKENS, D_MODEL)
    p_stack = p.reshape(DEPTH, TOKENS, PLE_DIM)
    cos_t, sin_t = _rope_tables(positions)
    sb_q_scale = jnp.concatenate([jnp.full((1, SB_WIDTH), Q_SCALE, F32),
                                  jnp.ones((1, (len(PLAIN_W_TILES) - 1) * PROJ_TN), F32)], axis=1)
    w_in0 = ev_w_in[0].astype(BF16)

    lambda_init = 0.8 - 0.6 * math.exp(-0.3 * 0)
    (proj, xn), (w_out0,) = _plain_proj(h, _row(ev_norm_pre[0]), w_in0, sb_q_scale,
                                        casts=[_Cast(ev_w_out, 0)])
    (qk,), (ffn_w2_0,) = _rope_proj(xn, w_in0, cos_t, sin_t, casts=[_Cast(ffn_w2, 0)])
    proj3 = proj.reshape(BATCH, SEQ, len(PLAIN_W_TILES) * PROJ_TN)
    qk3 = qk.reshape(BATCH, SEQ, 2 * DIFF_QK_WIDTH)
    (sb_o,), (ffn_w1_0,) = _sb_attention(proj3, casts=[_Cast(ffn_w1, 0)])
    lam_rows = jnp.stack([ev_lam_q1[0], ev_lam_k1[0], ev_lam_q2[0], ev_lam_k2[0]])
    (df_o,), (gate_0, pproj_0) = _diff_attention(
        qk3, proj3, lam_rows, ev_subln[0].reshape(DIFF_V_DIM, 1), lambda_init,
        casts=[_Cast(ple_w_gate, 0), _Cast(ple_w_proj, 0)])
    h = _out_proj([sb_o.reshape(TOKENS, SB_WIDTH), df_o.reshape(TOKENS, DIFF_V_WIDTH)], w_out0,
                  _row(ev_norm_post[0]), h, "even_out_proj")
    (h,), (od_in, od_out, ffn_w1_1, ffn_w2_1, gate_1, pproj_1) = _ffn(
        h, _row(ffn_norm_pre[0]), ffn_w1_0, ffn_w2_0, _row(ffn_norm_post[0]),
        casts=[_Cast(od_w_in, 0), _Cast(od_w_out, 0), _Cast(ffn_w1, 1), _Cast(ffn_w2, 1),
               _Cast(ple_w_gate, 1), _Cast(ple_w_proj, 1)])
    h = _ple(h, p_stack, 0, gate_0, pproj_0, _row(ple_norm[0]))

    uv = _odd_proj(h, _row(od_norm_pre[0]), od_in)
    y = _spatial_gate(uv, _row(od_ln_g[0]), _row(od_ln_b[0]), od_w_s, 0, od_b_s[0].T)
    h = _out_proj([y], od_out, _row(od_norm_post[0]), h, "odd_out_proj")
    (h,), _ = _ffn(h, _row(ffn_norm_pre[1]), ffn_w1_1, ffn_w2_1, _row(ffn_norm_post[1]), casts=[])
    h = _ple(h, p_stack, 1, gate_1, pproj_1, _row(ple_norm[1]))
    return h.reshape(BATCH, SEQ, D_MODEL)
```

```python
import functools
import math
from typing import NamedTuple

import jax
import jax.numpy as jnp
from jax import lax
from jax.experimental import pallas as pl
from jax.experimental.pallas import tpu as pltpu

D_MODEL = 2048
BATCH = 8
SEQ = 2048
DEPTH = 2
TOKENS = BATCH * SEQ

CHUNK = 64
HEAD_DIM = 128
SB_HEADS = 8
DIFF_HEADS = 4
DIFF_MAPS = 2 * DIFF_HEADS
DIFF_V_DIM = 2 * HEAD_DIM
ROT_DIM = HEAD_DIM // 4
ROT_HALF = ROT_DIM // 2
ROPE_THETA = 500000.0
SG_BLOCK = 128
SG_GROUPS = 16
SG_GROUP_DIM = 128
SG_WIDTH = SG_GROUPS * SG_GROUP_DIM
D_FF = 4 * D_MODEL
PLE_DIM = 256
SB_WIDTH = SB_HEADS * HEAD_DIM
DIFF_QK_WIDTH = DIFF_HEADS * 2 * HEAD_DIM
DIFF_V_WIDTH = DIFF_HEADS * DIFF_V_DIM
EVEN_IN_WIDTH = 3 * SB_WIDTH + 2 * DIFF_QK_WIDTH + DIFF_V_WIDTH
EPS = 1e-6
LOG2E = math.log2(math.e)
Q_SCALE = HEAD_DIM ** -0.5 * LOG2E
EXP2_ZERO_BELOW = -151.0

LANES = 128
BF16_SUBLANES = 16
VMEM_LIMIT_BYTES = 56 * 1024 * 1024

F32 = jnp.float32
BF16 = jnp.bfloat16

PROJ_TM = 1024
PROJ_TN = 1024
ATT_TQ = 256
ATT_BK = 256
SB_NH = 8
OUT_TM = 512
FFN_TM = 512
FFN_TF = 1024
PLE_TM = 512
SG_TM = 512
ROPE_TM = 2048
NORM_ROWS = 16

PLAIN_W_TILES = (0, 1, 2, 5)
ROPE_W_TILE0 = 3


def _chunk_of(idx):
    return jnp.right_shift(idx, CHUNK.bit_length() - 1)


def _rms_scale(x):
    return lax.rsqrt(jnp.mean(x * x, axis=-1, keepdims=True) + EPS)


class _Cast(NamedTuple):
    stack: jax.Array
    layer: int


def _no_rider():
    pass


def _call(body, *, name, grid, in_specs, out_specs, out_shape, args, scratch_shapes=(), casts=()):
    steps = math.prod(grid)
    n_in, n_out, n_cast = len(in_specs), len(out_specs), len(casts)

    def linear_step(*idx):
        step = idx[0]
        for extent, k in zip(grid[1:], idx[1:]):
            step = step * extent + k
        return step

    cast_in, cast_out, cast_shape = [], [], []
    for job in casts:
        _, rows, cols = job.stack.shape
        assert rows % BF16_SUBLANES == 0
        n_blocks = math.gcd(steps, rows // BF16_SUBLANES)
        blk_rows = rows // n_blocks
        repeat = steps // n_blocks
        cast_in.append(pl.BlockSpec(
            (None, blk_rows, cols),
            functools.partial(lambda *idx, layer, repeat: (layer, linear_step(*idx) // repeat, 0),
                              layer=job.layer, repeat=repeat)))
        cast_out.append(pl.BlockSpec(
            (blk_rows, cols),
            functools.partial(lambda *idx, repeat: (linear_step(*idx) // repeat, 0), repeat=repeat)))
        cast_shape.append(jax.ShapeDtypeStruct((rows, cols), BF16))

    def wrapped(*refs):
        ins = refs[:n_in]
        srcs = refs[n_in:n_in + n_cast]
        outs = refs[n_in + n_cast:n_in + n_cast + n_out]
        dsts = refs[n_in + n_cast + n_out:n_in + 2 * n_cast + n_out]
        scratch = refs[n_in + 2 * n_cast + n_out:]

        def rider():
            for src, dst in zip(srcs, dsts):
                dst[...] = src[...].astype(dst.dtype)

        body(*ins, *outs, *scratch, rider=rider if n_cast else _no_rider)

    results = pl.pallas_call(
        wrapped,
        grid=grid,
        in_specs=list(in_specs) + cast_in,
        out_specs=list(out_specs) + cast_out,
        out_shape=list(out_shape) + cast_shape,
        scratch_shapes=list(scratch_shapes),
        compiler_params=pltpu.CompilerParams(dimension_semantics=("arbitrary",) * len(grid),
                                             vmem_limit_bytes=VMEM_LIMIT_BYTES),
        name=name,
    )(*args, *[job.stack for job in casts])
    return results[:n_out], results[n_out:]


def _rope_table_kernel(pos_ref, freq_ref, cos_ref, sin_ref, *, rider):
    rider()
    ang = pos_ref[...].astype(F32) * freq_ref[...]
    lane = lax.broadcasted_iota(jnp.int32, ang.shape, 1)
    c = jnp.where(lane < ROT_DIM, jnp.cos(ang), 1.0)
    s = jnp.sin(ang)
    s = jnp.where(lane < ROT_HALF, -s, jnp.where(lane < ROT_DIM, s, 0.0))
    cos_ref[0] = c * Q_SCALE
    sin_ref[0] = s * Q_SCALE
    cos_ref[1] = c
    sin_ref[1] = s


def _rope_tables(positions):
    inv_freq = ROPE_THETA ** (-jnp.arange(0, ROT_DIM, 2, dtype=F32) / ROT_DIM)
    freq_row = jnp.concatenate(
        [inv_freq, inv_freq, jnp.zeros((LANES - ROT_DIM,), F32)]).reshape(1, LANES)
    pos = jnp.broadcast_to(positions.reshape(TOKENS, 1), (TOKENS, LANES))
    (cos_t, sin_t), _ = _call(
        _rope_table_kernel,
        name="rope_tables",
        grid=(TOKENS // ROPE_TM,),
        in_specs=[pl.BlockSpec((ROPE_TM, LANES), lambda i: (i, 0)),
                  pl.BlockSpec((1, LANES), lambda i: (0, 0))],
        out_specs=[pl.BlockSpec((2, ROPE_TM, LANES), lambda i: (0, i, 0)),
                   pl.BlockSpec((2, ROPE_TM, LANES), lambda i: (0, i, 0))],
        out_shape=[jax.ShapeDtypeStruct((2, TOKENS, LANES), F32)] * 2,
        args=(pos, freq_row))
    return cos_t, sin_t


def _normalize_rows(x_ref, g_ref, xn_ref):
    g = g_ref[...]
    for r0 in range(0, x_ref.shape[0], NORM_ROWS):
        x = x_ref[r0:r0 + NORM_ROWS, :]
        xn_ref[r0:r0 + NORM_ROWS, :] = (x * _rms_scale(x) * g).astype(BF16)


def _add_normalized_rows(h_ref, m_ref, g_ref, o_ref):
    g = g_ref[...]
    for r0 in range(0, h_ref.shape[0], NORM_ROWS):
        rows = slice(r0, r0 + NORM_ROWS)
        m = m_ref[rows, :]
        o_ref[rows, :] = h_ref[rows, :] + m * _rms_scale(m) * g


def _plain_proj_kernel(x_ref, g_ref, w_ref, cs_ref, o_ref, xn_ref, *, rider):
    @pl.when(pl.program_id(1) == 0)
    def _():
        _normalize_rows(x_ref, g_ref, xn_ref)

    rider()
    acc = jnp.dot(xn_ref[...], w_ref[...], preferred_element_type=F32)
    o_ref[...] = (acc * cs_ref[...]).astype(o_ref.dtype)


def _plain_proj(h, g, w, col_scale, casts):
    n_tiles = len(PLAIN_W_TILES)
    first_gap = PLAIN_W_TILES.index(5)

    def w_map(i, j):
        return (0, jnp.where(j < first_gap, j, j + (5 - first_gap)))

    return _call(
        _plain_proj_kernel,
        name="even_plain_proj",
        grid=(TOKENS // PROJ_TM, n_tiles),
        in_specs=[pl.BlockSpec((PROJ_TM, D_MODEL), lambda i, j: (i, 0)),
                  pl.BlockSpec((1, D_MODEL), lambda i, j: (0, 0)),
                  pl.BlockSpec((D_MODEL, PROJ_TN), w_map),
                  pl.BlockSpec((1, PROJ_TN), lambda i, j: (0, j))],
        out_specs=[pl.BlockSpec((PROJ_TM, PROJ_TN), lambda i, j: (i, j)),
                   pl.BlockSpec((PROJ_TM, D_MODEL), lambda i, j: (i, 0))],
        out_shape=[jax.ShapeDtypeStruct((TOKENS, n_tiles * PROJ_TN), BF16),
                   jax.ShapeDtypeStruct((TOKENS, D_MODEL), BF16)],
        args=(h, g, w, col_scale),
        casts=casts)


def _rope_proj_kernel(xn_ref, w_ref, cos_ref, sin_ref, o_ref, *, rider):
    rider()
    acc = jnp.dot(xn_ref[...], w_ref[...], preferred_element_type=F32)
    c = cos_ref[...]
    s = sin_ref[...]
    first = lax.broadcasted_iota(jnp.int32, c.shape, 1) < ROT_HALF
    for g in range(PROJ_TN // LANES):
        a = acc[:, g * LANES:(g + 1) * LANES]
        swapped = jnp.where(first, pltpu.roll(a, LANES - ROT_HALF, 1), pltpu.roll(a, ROT_HALF, 1))
        o_ref[:, g * LANES:(g + 1) * LANES] = (a * c + swapped * s).astype(o_ref.dtype)


def _rope_proj(xn, w, cos_t, sin_t, casts):
    return _call(
        _rope_proj_kernel,
        name="even_rope_proj",
        grid=(TOKENS // PROJ_TM, 2),
        in_specs=[pl.BlockSpec((PROJ_TM, D_MODEL), lambda i, j: (i, 0)),
                  pl.BlockSpec((D_MODEL, PROJ_TN), lambda i, j: (0, ROPE_W_TILE0 + j)),
                  pl.BlockSpec((None, PROJ_TM, LANES), lambda i, j: (j, i, 0)),
                  pl.BlockSpec((None, PROJ_TM, LANES), lambda i, j: (j, i, 0))],
        out_specs=[pl.BlockSpec((PROJ_TM, PROJ_TN), lambda i, j: (i, j))],
        out_shape=[jax.ShapeDtypeStruct((TOKENS, 2 * PROJ_TN), BF16)],
        args=(xn, w, cos_t, sin_t),
        casts=casts)


def _gelu_tanh(x):
    c = -2.0 * math.sqrt(2.0 / math.pi) * LOG2E
    return x / (1.0 + jnp.exp2(x * (c + (0.044715 * c) * (x * x))))


def _odd_proj_kernel(x_ref, g_ref, w_ref, o_ref, xn_ref, *, rider):
    @pl.when(pl.program_id(1) == 0)
    def _():
        _normalize_rows(x_ref, g_ref, xn_ref)

    rider()
    acc = jnp.dot(xn_ref[...], w_ref[...], preferred_element_type=F32)
    o_ref[...] = _gelu_tanh(acc).astype(o_ref.dtype)


def _odd_proj(h, g, w, casts):
    n = w.shape[1]
    return _call(
        _odd_proj_kernel,
        name="odd_in_proj",
        grid=(TOKENS // PROJ_TM, n // PROJ_TN),
        in_specs=[pl.BlockSpec((PROJ_TM, D_MODEL), lambda i, j: (i, 0)),
                  pl.BlockSpec((1, D_MODEL), lambda i, j: (0, 0)),
                  pl.BlockSpec((D_MODEL, PROJ_TN), lambda i, j: (0, j))],
        out_specs=[pl.BlockSpec((PROJ_TM, PROJ_TN), lambda i, j: (i, j))],
        out_shape=[jax.ShapeDtypeStruct((TOKENS, n), BF16)],
        scratch_shapes=[pltpu.VMEM((PROJ_TM, D_MODEL), BF16)],
        args=(h, g, w),
        casts=casts)


def _store_transposed_values(v_ref, vt_ref):
    for jb in range(SEQ // ATT_BK):
        vt_ref[jb] = v_ref[jb * ATT_BK:(jb + 1) * ATT_BK, :].astype(F32).T.astype(BF16)


def _scores_t(k_ref, q_ref, j, cols):
    kj = k_ref[pl.ds(j * ATT_BK, ATT_BK), cols]
    return lax.dot_general(kj, q_ref[:, cols], (((1,), (1,)), ((), ())),
                           preferred_element_type=F32)


def _sb_attn_kernel(q_ref, k_ref, v_ref, o_ref, vt_ref, acc_ref, carry_ref, *, rider):
    i = pl.program_id(2)

    @pl.when(i == 0)
    def _():
        _store_transposed_values(v_ref, vt_ref)

    rider()
    key = lax.broadcasted_iota(jnp.int32, (ATT_BK, ATT_TQ), 0)
    qry = lax.broadcasted_iota(jnp.int32, (ATT_BK, ATT_TQ), 1)
    causal = key < qry
    later_key = lax.broadcasted_iota(jnp.int32, (ATT_BK, ATT_BK), 1)
    this_key = lax.broadcasted_iota(jnp.int32, (ATT_BK, ATT_BK), 0)
    tri = (later_key > this_key).astype(BF16)

    heads = range(SB_NH)

    def head_cols(h):
        return slice(h * HEAD_DIM, (h + 1) * HEAD_DIM)

    def blocks(j, carries, diag):
        zs = [_scores_t(k_ref, q_ref, j, head_cols(h)) for h in heads]
        log_betas, log_keeps = [], []
        for z in zs:
            soft = jnp.log2(1.0 + jnp.exp2(-jnp.abs(z)))
            log_beta = jnp.minimum(z, 0.0) - soft
            log_keep = log_beta - z
            if diag:
                log_keep = jnp.where(causal, log_keep, 0.0)
            log_betas.append(log_beta)
            log_keeps.append(log_keep)
        laters = [jnp.dot(tri, lk.astype(BF16), preferred_element_type=F32) for lk in log_keeps]
        ws = []
        for h in heads:
            arg = log_betas[h] + laters[h]
            if carries is not None:
                arg = arg + carries[h]
            w = jnp.exp2(arg)
            if diag:
                w = jnp.where(causal, w, 0.0)
            ws.append(w.astype(BF16))
        contribs = [jnp.dot(vt_ref[j, head_cols(h), :], ws[h], preferred_element_type=F32)
                    for h in heads]
        colsums = [jnp.sum(lk, axis=0, keepdims=True) for lk in log_keeps]
        return contribs, colsums

    def any_weight_left(carries):
        worst = functools.reduce(jnp.maximum, carries)
        return (jnp.max(worst) > EXP2_ZERO_BELOW).astype(jnp.int32)

    contribs, colsums = blocks(i, None, True)
    for h in heads:
        acc_ref[h] = contribs[h]
        carry_ref[h] = colsums[h]

    def cond(state):
        j, more = state
        return jnp.logical_and(j >= 0, more > 0)

    def body(state):
        j, _ = state
        carries = [carry_ref[h] for h in heads]
        contribs, colsums = blocks(j, carries, False)
        carries = [carries[h] + colsums[h] for h in heads]
        for h in heads:
            acc_ref[h] += contribs[h]
            carry_ref[h] = carries[h]
        return j - 1, any_weight_left(carries)

    lax.while_loop(cond, body, (i - 1, any_weight_left(colsums)))
    for h in range(SB_NH):
        o_ref[:, h * HEAD_DIM:(h + 1) * HEAD_DIM] = acc_ref[h].T.astype(o_ref.dtype)


def _sb_attention(proj3, casts):
    width = SB_NH * HEAD_DIM
    groups = SB_HEADS // SB_NH
    return _call(
        _sb_attn_kernel,
        name="sb_attention",
        grid=(BATCH, groups, SEQ // ATT_TQ),
        in_specs=[pl.BlockSpec((None, ATT_TQ, width), lambda b, g, i: (b, i, g)),
                  pl.BlockSpec((None, SEQ, width), lambda b, g, i: (b, 0, groups + g)),
                  pl.BlockSpec((None, SEQ, width), lambda b, g, i: (b, 0, 2 * groups + g))],
        out_specs=[pl.BlockSpec((None, ATT_TQ, width), lambda b, g, i: (b, i, g))],
        out_shape=[jax.ShapeDtypeStruct((BATCH, SEQ, SB_WIDTH), BF16)],
        scratch_shapes=[pltpu.VMEM((SEQ // ATT_BK, width, ATT_BK), BF16),
                        pltpu.VMEM((SB_NH, HEAD_DIM, ATT_TQ), F32),
                        pltpu.VMEM((SB_NH, 1, ATT_TQ), F32)],
        args=(proj3, proj3, proj3),
        casts=casts)


def _diff_attn_kernel(lam_ref, g_ref, q_ref, k_ref, v_ref, o_ref, vt_ref, acc_ref, m_ref, l_ref,
                      z_ref, *, lambda_init, rider):
    i = pl.program_id(1)

    @pl.when(i == 0)
    def _():
        _store_transposed_values(v_ref, vt_ref)

    rider()
    key = lax.broadcasted_iota(jnp.int32, (ATT_BK, ATT_TQ), 0)
    qry = lax.broadcasted_iota(jnp.int32, (ATT_BK, ATT_TQ), 1)
    visible = _chunk_of(key) <= _chunk_of(qry)

    def head_cols(c):
        return slice(c * HEAD_DIM, (c + 1) * HEAD_DIM)

    def value_rows(c):
        return slice((c // 2) * DIFF_V_DIM, (c // 2 + 1) * DIFF_V_DIM)

    maps = range(DIFF_MAPS)

    def scores_all(j):
        return [_scores_t(k_ref, q_ref, j, head_cols(c)) for c in maps]

    def absorb(j):
        ps, alphas = [], []
        for c in maps:
            z = z_ref[c]
            m_old = m_ref[c]
            m_new = jnp.maximum(m_old, jnp.max(z, axis=0, keepdims=True))
            alpha = jnp.exp2(m_old - m_new)
            p = jnp.exp2(z - m_new)
            m_ref[c] = m_new
            l_ref[c] = l_ref[c] * alpha + jnp.sum(p, axis=0, keepdims=True)
            ps.append(p.astype(BF16))
            alphas.append(alpha)
        pvs = [jnp.dot(vt_ref[j, value_rows(c), :], ps[c], preferred_element_type=F32)
               for c in maps]
        for c in maps:
            acc_ref[c] = acc_ref[c] * alphas[c] + pvs[c]

    for c in maps:
        m_ref[c] = jnp.full((1, ATT_TQ), -jnp.inf, F32)
        l_ref[c] = jnp.zeros((1, ATT_TQ), F32)
        acc_ref[c] = jnp.zeros((DIFF_V_DIM, ATT_TQ), F32)
    for c, z in enumerate(scores_all(i)):
        z_ref[c] = jnp.where(visible, z, -jnp.inf)

    def body(jj, unused):
        j = i - jj
        zs_next = scores_all(j - 1)
        absorb(j)
        for c in maps:
            z_ref[c] = zs_next[c]
        return unused

    lax.fori_loop(0, i, body, 0)
    absorb(0)

    lam_rows = lam_ref[...]
    s1 = jnp.sum(lam_rows[0:1, :] * lam_rows[1:2, :], axis=-1, keepdims=True)
    s2 = jnp.sum(lam_rows[2:3, :] * lam_rows[3:4, :], axis=-1, keepdims=True)
    lam = jnp.exp(s1) - jnp.exp(s2) + lambda_init
    gain = g_ref[...]
    for h in range(DIFF_HEADS):
        o = acc_ref[2 * h] / l_ref[2 * h] - lam * (acc_ref[2 * h + 1] / l_ref[2 * h + 1])
        r = lax.rsqrt(jnp.mean(o * o, axis=0, keepdims=True) + EPS)
        o = o * r * gain * (1.0 - lambda_init)
        o_ref[:, h * DIFF_V_DIM:(h + 1) * DIFF_V_DIM] = o.T.astype(o_ref.dtype)


def _diff_attention(qk3, proj3, lam_rows, subln_col, lambda_init, casts):
    v_tile = PLAIN_W_TILES.index(5)
    return _call(
        functools.partial(_diff_attn_kernel, lambda_init=lambda_init),
        name="diff_attention",
        grid=(BATCH, SEQ // ATT_TQ),
        in_specs=[pl.BlockSpec((4, HEAD_DIM), lambda b, i: (0, 0)),
                  pl.BlockSpec((DIFF_V_DIM, 1), lambda b, i: (0, 0)),
                  pl.BlockSpec((None, ATT_TQ, DIFF_QK_WIDTH), lambda b, i: (b, i, 0)),
                  pl.BlockSpec((None, SEQ, DIFF_QK_WIDTH), lambda b, i: (b, 0, 1)),
                  pl.BlockSpec((None, SEQ, DIFF_V_WIDTH), lambda b, i: (b, 0, v_tile))],
        out_specs=[pl.BlockSpec((None, ATT_TQ, DIFF_V_WIDTH), lambda b, i: (b, i, 0))],
        out_shape=[jax.ShapeDtypeStruct((BATCH, SEQ, DIFF_V_WIDTH), BF16)],
        scratch_shapes=[pltpu.VMEM((SEQ // ATT_BK, DIFF_V_WIDTH, ATT_BK), BF16),
                        pltpu.VMEM((DIFF_MAPS, DIFF_V_DIM, ATT_TQ), F32),
                        pltpu.VMEM((DIFF_MAPS, 1, ATT_TQ), F32),
                        pltpu.VMEM((DIFF_MAPS, 1, ATT_TQ), F32),
                        pltpu.VMEM((DIFF_MAPS, ATT_BK, ATT_TQ), F32)],
        args=(lam_rows, subln_col, qk3, qk3, proj3),
        casts=casts)


def _out_proj_kernel(*refs, n_in, rider):
    a_refs = refs[:n_in]
    w_refs = refs[n_in:2 * n_in]
    g_ref, h_ref, o_ref = refs[2 * n_in:]
    rider()
    m = jnp.dot(a_refs[0][...], w_refs[0][...], preferred_element_type=F32)
    for a_ref, w_ref in zip(a_refs[1:], w_refs[1:]):
        m = m + jnp.dot(a_ref[...], w_ref[...], preferred_element_type=F32)
    o_ref[...] = m
    _add_normalized_rows(h_ref, o_ref, g_ref, o_ref)


def _out_proj(acts, w, g, h, name, casts=()):
    n_in = len(acts)
    k = w.shape[0] // n_in
    in_specs = ([pl.BlockSpec((OUT_TM, k), lambda i: (i, 0)) for _ in acts]
                + [pl.BlockSpec((k, D_MODEL), functools.partial(lambda i, r: (r, 0), r=r))
                   for r in range(n_in)]
                + [pl.BlockSpec((1, D_MODEL), lambda i: (0, 0)),
                   pl.BlockSpec((OUT_TM, D_MODEL), lambda i: (i, 0))])
    return _call(
        functools.partial(_out_proj_kernel, n_in=n_in),
        name=name,
        grid=(TOKENS // OUT_TM,),
        in_specs=in_specs,
        out_specs=[pl.BlockSpec((OUT_TM, D_MODEL), lambda i: (i, 0))],
        out_shape=[jax.ShapeDtypeStruct((TOKENS, D_MODEL), F32)],
        args=(*acts, *([w] * n_in), g, h),
        casts=casts)


def _ffn_kernel(h_ref, hnext_ref, g1_ref, w1_ref, w2_ref, g2_ref, o_ref, hn_even, hn_odd, *, rider):
    i = pl.program_id(0)
    f = pl.program_id(1)
    n_f = D_FF // FFN_TF
    rows_per_step = FFN_TM // n_f
    g1 = g1_ref[...]

    @pl.when(jnp.logical_and(i == 0, f == 0))
    def _():
        _normalize_rows(h_ref, g1_ref, hn_even)

    def step(cur_ref, nxt_ref, first):
        rider()
        for c in range(rows_per_step // NORM_ROWS):
            rows = pl.ds(pl.multiple_of(f * rows_per_step + c * NORM_ROWS, NORM_ROWS), NORM_ROWS)
            x = hnext_ref[rows, :]
            nxt_ref[rows, :] = (x * _rms_scale(x) * g1).astype(BF16)
        a = jnp.maximum(jnp.dot(cur_ref[...], w1_ref[...], preferred_element_type=F32), 0.0)
        part = jnp.dot((a * a).astype(BF16), w2_ref[...], preferred_element_type=F32)
        if first:
            o_ref[...] = part
        else:
            o_ref[...] += part

    even_tile = lax.rem(i, 2) == 0
    for is_even, cur_ref, nxt_ref in ((True, hn_even, hn_odd), (False, hn_odd, hn_even)):
        for first in (True, False):
            cond = jnp.logical_and(even_tile == is_even, (f == 0) == first)
            pl.when(cond)(functools.partial(step, cur_ref, nxt_ref, first))

    @pl.when(f == n_f - 1)
    def _():
        _add_normalized_rows(h_ref, o_ref, g2_ref, o_ref)


def _ffn(h, g1, w1, w2, g2, casts):
    n_tiles = TOKENS // FFN_TM
    return _call(
        _ffn_kernel,
        name="ffn",
        grid=(n_tiles, D_FF // FFN_TF),
        in_specs=[pl.BlockSpec((FFN_TM, D_MODEL), lambda i, f: (i, 0)),
                  pl.BlockSpec((FFN_TM, D_MODEL), lambda i, f: (jnp.minimum(i + 1, n_tiles - 1), 0)),
                  pl.BlockSpec((1, D_MODEL), lambda i, f: (0, 0)),
                  pl.BlockSpec((D_MODEL, FFN_TF), lambda i, f: (0, f)),
                  pl.BlockSpec((FFN_TF, D_MODEL), lambda i, f: (f, 0)),
                  pl.BlockSpec((1, D_MODEL), lambda i, f: (0, 0))],
        out_specs=[pl.BlockSpec((FFN_TM, D_MODEL), lambda i, f: (i, 0))],
        out_shape=[jax.ShapeDtypeStruct((TOKENS, D_MODEL), F32)],
        scratch_shapes=[pltpu.VMEM((FFN_TM, D_MODEL), BF16), pltpu.VMEM((FFN_TM, D_MODEL), BF16)],
        args=(h, h, g1, w1, w2, g2),
        casts=casts)


def _ple_kernel(h_ref, p_ref, wg_ref, wp_ref, g_ref, o_ref, *, rider):
    rider()
    z = jnp.dot(h_ref[...].astype(BF16), wg_ref[...], preferred_element_type=F32)
    e = jnp.dot(p_ref[...].astype(BF16), wp_ref[...], preferred_element_type=F32)
    o_ref[...] = e / (1.0 + jnp.exp2(z * (-LOG2E)))
    _add_normalized_rows(h_ref, o_ref, g_ref, o_ref)


def _ple(h, p_stack, layer, wg, wp, g, casts=()):
    return _call(
        _ple_kernel,
        name="ple",
        grid=(TOKENS // PLE_TM,),
        in_specs=[pl.BlockSpec((PLE_TM, D_MODEL), lambda i: (i, 0)),
                  pl.BlockSpec((None, PLE_TM, PLE_DIM), lambda i: (layer, i, 0)),
                  pl.BlockSpec((D_MODEL, D_MODEL), lambda i: (0, 0)),
                  pl.BlockSpec((PLE_DIM, D_MODEL), lambda i: (0, 0)),
                  pl.BlockSpec((1, D_MODEL), lambda i: (0, 0))],
        out_specs=[pl.BlockSpec((PLE_TM, D_MODEL), lambda i: (i, 0))],
        out_shape=[jax.ShapeDtypeStruct((TOKENS, D_MODEL), F32)],
        args=(h, p_stack, wg, wp, g),
        casts=casts)


def _sg_kernel(u_ref, v_ref, lng_ref, lnb_ref, ws_ref, bst_ref, y_ref, vn_ref, *, rider):
    rider()
    v = v_ref[...].astype(F32)
    mu = jnp.mean(v, axis=-1, keepdims=True)
    vc = v - mu
    inv = lax.rsqrt(jnp.mean(vc * vc, axis=-1, keepdims=True) + EPS)
    vn_ref[...] = (vc * inv * lng_ref[...] + lnb_ref[...]).astype(BF16)

    n_blk = SG_TM // SG_BLOCK
    t_idx = lax.broadcasted_iota(jnp.int32, (SG_BLOCK, SG_BLOCK), 0)
    s_idx = lax.broadcasted_iota(jnp.int32, (SG_BLOCK, SG_BLOCK), 1)
    visible = _chunk_of(s_idx) <= _chunk_of(t_idx)
    bst = bst_ref[...]
    for g in range(SG_GROUPS):
        cols = slice(g * SG_GROUP_DIM, (g + 1) * SG_GROUP_DIM)
        w = jnp.where(visible, ws_ref[g], 0.0).astype(BF16)
        rhs = jnp.concatenate(
            [vn_ref[n * SG_BLOCK:(n + 1) * SG_BLOCK, cols] for n in range(n_blk)], axis=1)
        mixed = jnp.dot(w, rhs, preferred_element_type=F32) + bst[:, g:g + 1]
        for n in range(n_blk):
            rows = slice(n * SG_BLOCK, (n + 1) * SG_BLOCK)
            u = u_ref[rows, cols].astype(F32)
            y_ref[rows, cols] = (u * mixed[:, n * SG_BLOCK:(n + 1) * SG_BLOCK]).astype(y_ref.dtype)


def _spatial_gate(uv, ln_g, ln_b, w_s_stack, layer, b_s_t, casts):
    return _call(
        _sg_kernel,
        name="spatial_gate",
        grid=(TOKENS // SG_TM,),
        in_specs=[pl.BlockSpec((SG_TM, SG_WIDTH), lambda i: (i, 0)),
                  pl.BlockSpec((SG_TM, SG_WIDTH), lambda i: (i, 1)),
                  pl.BlockSpec((1, SG_WIDTH), lambda i: (0, 0)),
                  pl.BlockSpec((1, SG_WIDTH), lambda i: (0, 0)),
                  pl.BlockSpec((None, SG_GROUPS, SG_BLOCK, SG_BLOCK), lambda i: (layer, 0, 0, 0)),
                  pl.BlockSpec((SG_BLOCK, SG_GROUPS), lambda i: (0, 0))],
        out_specs=[pl.BlockSpec((SG_TM, SG_WIDTH), lambda i: (i, 0))],
        out_shape=[jax.ShapeDtypeStruct((TOKENS, SG_WIDTH), BF16)],
        scratch_shapes=[pltpu.VMEM((SG_TM, SG_WIDTH), BF16)],
        args=(uv, uv, ln_g, ln_b, w_s_stack, b_s_t),
        casts=casts)


def _row(v):
    return v.reshape(1, -1)


def kernel(x, p, positions, ev_norm_pre, ev_w_in, ev_lam_q1, ev_lam_k1, ev_lam_q2, ev_lam_k2,
           ev_subln, ev_w_out, ev_norm_post, od_norm_pre, od_w_in, od_ln_g, od_ln_b, od_w_s,
           od_b_s, od_w_out, od_norm_post, ffn_norm_pre, ffn_w1, ffn_w2, ffn_norm_post,
           ple_w_proj, ple_w_gate, ple_norm):
    assert DEPTH == 2, "the cast schedule below is written for one even and one odd layer"
    h = x.reshape(TOKENS, D_MODEL)
    p_stack = p.reshape(DEPTH, TOKENS, PLE_DIM)
    cos_t, sin_t = _rope_tables(positions)
    sb_q_scale = jnp.concatenate([jnp.full((1, SB_WIDTH), Q_SCALE, F32),
                                  jnp.ones((1, (len(PLAIN_W_TILES) - 1) * PROJ_TN), F32)], axis=1)
    w_in0 = ev_w_in[0].astype(BF16)

    lambda_init = 0.8 - 0.6 * math.exp(-0.3 * 0)
    (proj, xn), (w_out0,) = _plain_proj(h, _row(ev_norm_pre[0]), w_in0, sb_q_scale,
                                        casts=[_Cast(ev_w_out, 0)])
    (qk,), (ffn_w2_0,) = _rope_proj(xn, w_in0, cos_t, sin_t, casts=[_Cast(ffn_w2, 0)])
    proj3 = proj.reshape(BATCH, SEQ, len(PLAIN_W_TILES) * PROJ_TN)
    qk3 = qk.reshape(BATCH, SEQ, 2 * DIFF_QK_WIDTH)
    (sb_o,), (ffn_w1_0,) = _sb_attention(proj3, casts=[_Cast(ffn_w1, 0)])
    lam_rows = jnp.stack([ev_lam_q1[0], ev_lam_k1[0], ev_lam_q2[0], ev_lam_k2[0]])
    (df_o,), (gate_0, pproj_0) = _diff_attention(
        qk3, proj3, lam_rows, ev_subln[0].reshape(DIFF_V_DIM, 1), lambda_init,
        casts=[_Cast(ple_w_gate, 0), _Cast(ple_w_proj, 0)])
    (h,), _ = _out_proj([sb_o.reshape(TOKENS, SB_WIDTH), df_o.reshape(TOKENS, DIFF_V_WIDTH)],
                        w_out0, _row(ev_norm_post[0]), h, "even_out_proj")
    (h,), _ = _ffn(h, _row(ffn_norm_pre[0]), ffn_w1_0, ffn_w2_0, _row(ffn_norm_post[0]), casts=[])
    (h,), (od_in, od_out) = _ple(h, p_stack, 0, gate_0, pproj_0, _row(ple_norm[0]),
                                 casts=[_Cast(od_w_in, 0), _Cast(od_w_out, 0)])

    (uv,), (ffn_w1_1,) = _odd_proj(h, _row(od_norm_pre[0]), od_in, casts=[_Cast(ffn_w1, 1)])
    (y,), (gate_1, pproj_1) = _spatial_gate(
        uv, _row(od_ln_g[0]), _row(od_ln_b[0]), od_w_s, 0, od_b_s[0].T,
        casts=[_Cast(ple_w_gate, 1), _Cast(ple_w_proj, 1)])
    (h,), (ffn_w2_1,) = _out_proj([y], od_out, _row(od_norm_post[0]), h, "odd_out_proj",
                                  casts=[_Cast(ffn_w2, 1)])
    (h,), _ = _ffn(h, _row(ffn_norm_pre[1]), ffn_w1_1, ffn_w2_1, _row(ffn_norm_post[1]), casts=[])
    (h,), _ = _ple(h, p_stack, 1, gate_1, pproj_1, _row(ple_norm[1]))
    return h.reshape(BATCH, SEQ, D_MODEL)
```

```python
import functools
import math
from typing import NamedTuple

import jax
import jax.numpy as jnp
from jax import lax
from jax.experimental import pallas as pl
from jax.experimental.pallas import tpu as pltpu

D_MODEL = 2048
BATCH = 8
SEQ = 2048
DEPTH = 2
TOKENS = BATCH * SEQ

CHUNK = 64
HEAD_DIM = 128
SB_HEADS = 8
DIFF_HEADS = 4
DIFF_MAPS = 2 * DIFF_HEADS
DIFF_V_DIM = 2 * HEAD_DIM
ROT_DIM = HEAD_DIM // 4
ROT_HALF = ROT_DIM // 2
ROPE_THETA = 500000.0
SG_BLOCK = 128
SG_GROUPS = 16
SG_GROUP_DIM = 128
SG_WIDTH = SG_GROUPS * SG_GROUP_DIM
D_FF = 4 * D_MODEL
PLE_DIM = 256
SB_WIDTH = SB_HEADS * HEAD_DIM
DIFF_QK_WIDTH = DIFF_HEADS * 2 * HEAD_DIM
DIFF_V_WIDTH = DIFF_HEADS * DIFF_V_DIM
EVEN_IN_WIDTH = 3 * SB_WIDTH + 2 * DIFF_QK_WIDTH + DIFF_V_WIDTH
EPS = 1e-6
LOG2E = math.log2(math.e)
Q_SCALE = HEAD_DIM ** -0.5 * LOG2E
EXP2_ZERO_BELOW = -151.0

LANES = 128
BF16_SUBLANES = 16
VMEM_LIMIT_BYTES = 56 * 1024 * 1024

F32 = jnp.float32
BF16 = jnp.bfloat16

PROJ_TM = 1024
PROJ_TN = 1024
ATT_TQ = 256
ATT_BK = 256
SB_NH = 8
OUT_TM = 512
FFN_TM = 512
FFN_TF = 1024
PLE_TM = 512
SG_TM = 512
ROPE_TM = 2048
NORM_ROWS = 16

PLAIN_W_TILES = (0, 1, 2, 5)
ROPE_W_TILE0 = 3


def _chunk_of(idx):
    return jnp.right_shift(idx, CHUNK.bit_length() - 1)


def _rms_scale(x):
    return lax.rsqrt(jnp.mean(x * x, axis=-1, keepdims=True) + EPS)


class _Cast(NamedTuple):
    stack: jax.Array
    layer: int
    col_tile: int | None = None


def _no_rider():
    pass


def _call(body, *, name, grid, in_specs, out_specs, out_shape, args, scratch_shapes=(), casts=()):
    steps = math.prod(grid)
    n_in, n_out, n_cast = len(in_specs), len(out_specs), len(casts)

    def linear_step(*idx):
        step = idx[0]
        for extent, k in zip(grid[1:], idx[1:]):
            step = step * extent + k
        return step

    cast_in, cast_out, cast_shape = [], [], []
    for job in casts:
        _, rows, cols = job.stack.shape
        assert rows % BF16_SUBLANES == 0
        n_blocks = math.gcd(steps, rows // BF16_SUBLANES)
        blk_rows = rows // n_blocks
        repeat = steps // n_blocks
        cast_in.append(pl.BlockSpec(
            (None, blk_rows, cols),
            functools.partial(lambda *idx, layer, repeat: (layer, linear_step(*idx) // repeat, 0),
                              layer=job.layer, repeat=repeat)))
        if job.col_tile is None:
            cast_out.append(pl.BlockSpec(
                (blk_rows, cols),
                functools.partial(lambda *idx, repeat: (linear_step(*idx) // repeat, 0),
                                  repeat=repeat)))
            cast_shape.append(jax.ShapeDtypeStruct((rows, cols), BF16))
        else:
            n_col_tiles = cols // job.col_tile
            cast_out.append(pl.BlockSpec(
                (n_col_tiles, blk_rows, job.col_tile),
                functools.partial(lambda *idx, repeat: (0, linear_step(*idx) // repeat, 0),
                                  repeat=repeat)))
            cast_shape.append(jax.ShapeDtypeStruct((n_col_tiles, rows, job.col_tile), BF16))

    def wrapped(*refs):
        ins = refs[:n_in]
        srcs = refs[n_in:n_in + n_cast]
        outs = refs[n_in + n_cast:n_in + n_cast + n_out]
        dsts = refs[n_in + n_cast + n_out:n_in + 2 * n_cast + n_out]
        scratch = refs[n_in + 2 * n_cast + n_out:]

        def rider():
            for job, src, dst in zip(casts, srcs, dsts):
                if job.col_tile is None:
                    dst[...] = src[...].astype(dst.dtype)
                else:
                    for t in range(dst.shape[0]):
                        dst[t] = src[:, t * job.col_tile:(t + 1) * job.col_tile].astype(dst.dtype)

        body(*ins, *outs, *scratch, rider=rider if n_cast else _no_rider)

    results = pl.pallas_call(
        wrapped,
        grid=grid,
        in_specs=list(in_specs) + cast_in,
        out_specs=list(out_specs) + cast_out,
        out_shape=list(out_shape) + cast_shape,
        scratch_shapes=list(scratch_shapes),
        compiler_params=pltpu.CompilerParams(dimension_semantics=("arbitrary",) * len(grid),
                                             vmem_limit_bytes=VMEM_LIMIT_BYTES),
        name=name,
    )(*args, *[job.stack for job in casts])
    return results[:n_out], results[n_out:]


def _rope_table_kernel(pos_ref, freq_ref, cos_ref, sin_ref, *, rider):
    rider()
    ang = pos_ref[...].astype(F32) * freq_ref[...]
    lane = lax.broadcasted_iota(jnp.int32, ang.shape, 1)
    c = jnp.where(lane < ROT_DIM, jnp.cos(ang), 1.0)
    s = jnp.sin(ang)
    s = jnp.where(lane < ROT_HALF, -s, jnp.where(lane < ROT_DIM, s, 0.0))
    cos_ref[0] = c * Q_SCALE
    sin_ref[0] = s * Q_SCALE
    cos_ref[1] = c
    sin_ref[1] = s


def _rope_tables(positions):
    inv_freq = ROPE_THETA ** (-jnp.arange(0, ROT_DIM, 2, dtype=F32) / ROT_DIM)
    freq_row = jnp.concatenate(
        [inv_freq, inv_freq, jnp.zeros((LANES - ROT_DIM,), F32)]).reshape(1, LANES)
    pos = jnp.broadcast_to(positions.reshape(TOKENS, 1), (TOKENS, LANES))
    (cos_t, sin_t), _ = _call(
        _rope_table_kernel,
        name="rope_tables",
        grid=(TOKENS // ROPE_TM,),
        in_specs=[pl.BlockSpec((ROPE_TM, LANES), lambda i: (i, 0)),
                  pl.BlockSpec((1, LANES), lambda i: (0, 0))],
        out_specs=[pl.BlockSpec((2, ROPE_TM, LANES), lambda i: (0, i, 0)),
                   pl.BlockSpec((2, ROPE_TM, LANES), lambda i: (0, i, 0))],
        out_shape=[jax.ShapeDtypeStruct((2, TOKENS, LANES), F32)] * 2,
        args=(pos, freq_row))
    return cos_t, sin_t


def _normalize_rows(x_ref, g_ref, xn_ref):
    g = g_ref[...]
    for r0 in range(0, x_ref.shape[0], NORM_ROWS):
        x = x_ref[r0:r0 + NORM_ROWS, :]
        xn_ref[r0:r0 + NORM_ROWS, :] = (x * _rms_scale(x) * g).astype(BF16)


def _add_normalized_rows(h_ref, m_ref, g_ref, o_ref):
    g = g_ref[...]
    for r0 in range(0, h_ref.shape[0], NORM_ROWS):
        rows = slice(r0, r0 + NORM_ROWS)
        m = m_ref[rows, :]
        o_ref[rows, :] = h_ref[rows, :] + m * _rms_scale(m) * g


def _plain_proj_kernel(x_ref, g_ref, w_ref, cs_ref, o_ref, xn_ref, *, rider):
    @pl.when(pl.program_id(1) == 0)
    def _():
        _normalize_rows(x_ref, g_ref, xn_ref)

    rider()
    acc = jnp.dot(xn_ref[...], w_ref[...], preferred_element_type=F32)
    o_ref[...] = (acc * cs_ref[...]).astype(o_ref.dtype)


def _plain_proj(h, g, w, col_scale, casts):
    n_tiles = len(PLAIN_W_TILES)
    first_gap = PLAIN_W_TILES.index(5)

    def w_map(i, j):
        return (jnp.where(j < first_gap, j, j + (5 - first_gap)), 0, 0)

    return _call(
        _plain_proj_kernel,
        name="even_plain_proj",
        grid=(TOKENS // PROJ_TM, n_tiles),
        in_specs=[pl.BlockSpec((PROJ_TM, D_MODEL), lambda i, j: (i, 0)),
                  pl.BlockSpec((1, D_MODEL), lambda i, j: (0, 0)),
                  pl.BlockSpec((None, D_MODEL, PROJ_TN), w_map),
                  pl.BlockSpec((1, PROJ_TN), lambda i, j: (0, j))],
        out_specs=[pl.BlockSpec((PROJ_TM, PROJ_TN), lambda i, j: (i, j)),
                   pl.BlockSpec((PROJ_TM, D_MODEL), lambda i, j: (i, 0))],
        out_shape=[jax.ShapeDtypeStruct((TOKENS, n_tiles * PROJ_TN), BF16),
                   jax.ShapeDtypeStruct((TOKENS, D_MODEL), BF16)],
        args=(h, g, w, col_scale),
        casts=casts)


def _rope_proj_kernel(xn_ref, w_ref, cos_ref, sin_ref, o_ref, *, rider):
    rider()
    acc = jnp.dot(xn_ref[...], w_ref[...], preferred_element_type=F32)
    c = cos_ref[...]
    s = sin_ref[...]
    first = lax.broadcasted_iota(jnp.int32, c.shape, 1) < ROT_HALF
    for g in range(PROJ_TN // LANES):
        a = acc[:, g * LANES:(g + 1) * LANES]
        swapped = jnp.where(first, pltpu.roll(a, LANES - ROT_HALF, 1), pltpu.roll(a, ROT_HALF, 1))
        o_ref[:, g * LANES:(g + 1) * LANES] = (a * c + swapped * s).astype(o_ref.dtype)


def _rope_proj(xn, w, cos_t, sin_t, casts):
    return _call(
        _rope_proj_kernel,
        name="even_rope_proj",
        grid=(TOKENS // PROJ_TM, 2),
        in_specs=[pl.BlockSpec((PROJ_TM, D_MODEL), lambda i, j: (i, 0)),
                  pl.BlockSpec((None, D_MODEL, PROJ_TN), lambda i, j: (ROPE_W_TILE0 + j, 0, 0)),
                  pl.BlockSpec((None, PROJ_TM, LANES), lambda i, j: (j, i, 0)),
                  pl.BlockSpec((None, PROJ_TM, LANES), lambda i, j: (j, i, 0))],
        out_specs=[pl.BlockSpec((PROJ_TM, PROJ_TN), lambda i, j: (i, j))],
        out_shape=[jax.ShapeDtypeStruct((TOKENS, 2 * PROJ_TN), BF16)],
        args=(xn, w, cos_t, sin_t),
        casts=casts)


def _gelu_tanh(x):
    c = -2.0 * math.sqrt(2.0 / math.pi) * LOG2E
    return x / (1.0 + jnp.exp2(x * (c + (0.044715 * c) * (x * x))))


def _odd_proj_kernel(x_ref, g_ref, w_ref, o_ref, xn_ref, *, rider):
    @pl.when(pl.program_id(1) == 0)
    def _():
        _normalize_rows(x_ref, g_ref, xn_ref)

    rider()
    acc = jnp.dot(xn_ref[...], w_ref[...], preferred_element_type=F32)
    o_ref[...] = _gelu_tanh(acc).astype(o_ref.dtype)


def _odd_proj(h, g, w, casts):
    n = w.shape[0] * PROJ_TN
    return _call(
        _odd_proj_kernel,
        name="odd_in_proj",
        grid=(TOKENS // PROJ_TM, n // PROJ_TN),
        in_specs=[pl.BlockSpec((PROJ_TM, D_MODEL), lambda i, j: (i, 0)),
                  pl.BlockSpec((1, D_MODEL), lambda i, j: (0, 0)),
                  pl.BlockSpec((None, D_MODEL, PROJ_TN), lambda i, j: (j, 0, 0))],
        out_specs=[pl.BlockSpec((PROJ_TM, PROJ_TN), lambda i, j: (i, j))],
        out_shape=[jax.ShapeDtypeStruct((TOKENS, n), BF16)],
        scratch_shapes=[pltpu.VMEM((PROJ_TM, D_MODEL), BF16)],
        args=(h, g, w),
        casts=casts)


def _store_transposed_values(v_ref, vt_ref):
    for jb in range(SEQ // ATT_BK):
        vt_ref[jb] = v_ref[jb * ATT_BK:(jb + 1) * ATT_BK, :].astype(F32).T.astype(BF16)


def _scores_t(k_ref, q_ref, j, cols):
    kj = k_ref[pl.ds(j * ATT_BK, ATT_BK), cols]
    return lax.dot_general(kj, q_ref[:, cols], (((1,), (1,)), ((), ())),
                           preferred_element_type=F32)


def _sb_attn_kernel(q_ref, k_ref, v_ref, o_ref, vt_ref, acc_ref, carry_ref, *, rider):
    i = pl.program_id(2)

    @pl.when(i == 0)
    def _():
        _store_transposed_values(v_ref, vt_ref)

    rider()
    key = lax.broadcasted_iota(jnp.int32, (ATT_BK, ATT_TQ), 0)
    qry = lax.broadcasted_iota(jnp.int32, (ATT_BK, ATT_TQ), 1)
    causal = key < qry
    later_key = lax.broadcasted_iota(jnp.int32, (ATT_BK, ATT_BK), 1)
    this_key = lax.broadcasted_iota(jnp.int32, (ATT_BK, ATT_BK), 0)
    tri = (later_key > this_key).astype(BF16)

    heads = range(SB_NH)

    def head_cols(h):
        return slice(h * HEAD_DIM, (h + 1) * HEAD_DIM)

    def blocks(j, carries, diag):
        zs = [_scores_t(k_ref, q_ref, j, head_cols(h)) for h in heads]
        log_betas, log_keeps = [], []
        for z in zs:
            soft = jnp.log2(1.0 + jnp.exp2(-jnp.abs(z)))
            log_beta = jnp.minimum(z, 0.0) - soft
            log_keep = log_beta - z
            if diag:
                log_keep = jnp.where(causal, log_keep, 0.0)
            log_betas.append(log_beta)
            log_keeps.append(log_keep)
        laters = [jnp.dot(tri, lk.astype(BF16), preferred_element_type=F32) for lk in log_keeps]
        ws = []
        for h in heads:
            arg = log_betas[h] + laters[h]
            if carries is not None:
                arg = arg + carries[h]
            w = jnp.exp2(arg)
            if diag:
                w = jnp.where(causal, w, 0.0)
            ws.append(w.astype(BF16))
        contribs = [jnp.dot(vt_ref[j, head_cols(h), :], ws[h], preferred_element_type=F32)
                    for h in heads]
        colsums = [jnp.sum(lk, axis=0, keepdims=True) for lk in log_keeps]
        return contribs, colsums

    def any_weight_left(carries):
        worst = functools.reduce(jnp.maximum, carries)
        return (jnp.max(worst) > EXP2_ZERO_BELOW).astype(jnp.int32)

    contribs, colsums = blocks(i, None, True)
    for h in heads:
        acc_ref[h] = contribs[h]
        carry_ref[h] = colsums[h]

    def cond(state):
        j, more = state
        return jnp.logical_and(j >= 0, more > 0)

    def body(state):
        j, _ = state
        carries = [carry_ref[h] for h in heads]
        contribs, colsums = blocks(j, carries, False)
        carries = [carries[h] + colsums[h] for h in heads]
        for h in heads:
            acc_ref[h] += contribs[h]
            carry_ref[h] = carries[h]
        return j - 1, any_weight_left(carries)

    lax.while_loop(cond, body, (i - 1, any_weight_left(colsums)))
    for h in range(SB_NH):
        o_ref[:, h * HEAD_DIM:(h + 1) * HEAD_DIM] = acc_ref[h].T.astype(o_ref.dtype)


def _sb_attention(proj3, casts):
    width = SB_NH * HEAD_DIM
    groups = SB_HEADS // SB_NH
    return _call(
        _sb_attn_kernel,
        name="sb_attention",
        grid=(BATCH, groups, SEQ // ATT_TQ),
        in_specs=[pl.BlockSpec((None, ATT_TQ, width), lambda b, g, i: (b, i, g)),
                  pl.BlockSpec((None, SEQ, width), lambda b, g, i: (b, 0, groups + g)),
                  pl.BlockSpec((None, SEQ, width), lambda b, g, i: (b, 0, 2 * groups + g))],
        out_specs=[pl.BlockSpec((None, ATT_TQ, width), lambda b, g, i: (b, i, g))],
        out_shape=[jax.ShapeDtypeStruct((BATCH, SEQ, SB_WIDTH), BF16)],
        scratch_shapes=[pltpu.VMEM((SEQ // ATT_BK, width, ATT_BK), BF16),
                        pltpu.VMEM((SB_NH, HEAD_DIM, ATT_TQ), F32),
                        pltpu.VMEM((SB_NH, 1, ATT_TQ), F32)],
        args=(proj3, proj3, proj3),
        casts=casts)


def _diff_attn_kernel(lam_ref, g_ref, q_ref, k_ref, v_ref, o_ref, vt_ref, acc_ref, m_ref, l_ref,
                      z_ref, *, lambda_init, rider):
    i = pl.program_id(1)

    @pl.when(i == 0)
    def _():
        _store_transposed_values(v_ref, vt_ref)

    rider()
    key = lax.broadcasted_iota(jnp.int32, (ATT_BK, ATT_TQ), 0)
    qry = lax.broadcasted_iota(jnp.int32, (ATT_BK, ATT_TQ), 1)
    visible = _chunk_of(key) <= _chunk_of(qry)

    def head_cols(c):
        return slice(c * HEAD_DIM, (c + 1) * HEAD_DIM)

    def value_rows(c):
        return slice((c // 2) * DIFF_V_DIM, (c // 2 + 1) * DIFF_V_DIM)

    maps = range(DIFF_MAPS)

    def scores_all(j):
        return [_scores_t(k_ref, q_ref, j, head_cols(c)) for c in maps]

    def absorb(j):
        ps, alphas = [], []
        for c in maps:
            z = z_ref[c]
            m_old = m_ref[c]
            m_new = jnp.maximum(m_old, jnp.max(z, axis=0, keepdims=True))
            alpha = jnp.exp2(m_old - m_new)
            p = jnp.exp2(z - m_new)
            m_ref[c] = m_new
            l_ref[c] = l_ref[c] * alpha + jnp.sum(p, axis=0, keepdims=True)
            ps.append(p.astype(BF16))
            alphas.append(alpha)
        pvs = [jnp.dot(vt_ref[j, value_rows(c), :], ps[c], preferred_element_type=F32)
               for c in maps]
        for c in maps:
            acc_ref[c] = acc_ref[c] * alphas[c] + pvs[c]

    for c in maps:
        m_ref[c] = jnp.full((1, ATT_TQ), -jnp.inf, F32)
        l_ref[c] = jnp.zeros((1, ATT_TQ), F32)
        acc_ref[c] = jnp.zeros((DIFF_V_DIM, ATT_TQ), F32)
    for c, z in enumerate(scores_all(i)):
        z_ref[c] = jnp.where(visible, z, -jnp.inf)

    def body(jj, unused):
        j = i - jj
        zs_next = scores_all(j - 1)
        absorb(j)
        for c in maps:
            z_ref[c] = zs_next[c]
        return unused

    lax.fori_loop(0, i, body, 0)
    absorb(0)

    lam_rows = lam_ref[...]
    s1 = jnp.sum(lam_rows[0:1, :] * lam_rows[1:2, :], axis=-1, keepdims=True)
    s2 = jnp.sum(lam_rows[2:3, :] * lam_rows[3:4, :], axis=-1, keepdims=True)
    lam = jnp.exp(s1) - jnp.exp(s2) + lambda_init
    gain = g_ref[...]
    for h in range(DIFF_HEADS):
        o = acc_ref[2 * h] / l_ref[2 * h] - lam * (acc_ref[2 * h + 1] / l_ref[2 * h + 1])
        r = lax.rsqrt(jnp.mean(o * o, axis=0, keepdims=True) + EPS)
        o = o * r * gain * (1.0 - lambda_init)
        o_ref[:, h * DIFF_V_DIM:(h + 1) * DIFF_V_DIM] = o.T.astype(o_ref.dtype)


def _diff_attention(qk3, proj3, lam_rows, subln_col, lambda_init, casts):
    v_tile = PLAIN_W_TILES.index(5)
    return _call(
        functools.partial(_diff_attn_kernel, lambda_init=lambda_init),
        name="diff_attention",
        grid=(BATCH, SEQ // ATT_TQ),
        in_specs=[pl.BlockSpec((4, HEAD_DIM), lambda b, i: (0, 0)),
                  pl.BlockSpec((DIFF_V_DIM, 1), lambda b, i: (0, 0)),
                  pl.BlockSpec((None, ATT_TQ, DIFF_QK_WIDTH), lambda b, i: (b, i, 0)),
                  pl.BlockSpec((None, SEQ, DIFF_QK_WIDTH), lambda b, i: (b, 0, 1)),
                  pl.BlockSpec((None, SEQ, DIFF_V_WIDTH), lambda b, i: (b, 0, v_tile))],
        out_specs=[pl.BlockSpec((None, ATT_TQ, DIFF_V_WIDTH), lambda b, i: (b, i, 0))],
        out_shape=[jax.ShapeDtypeStruct((BATCH, SEQ, DIFF_V_WIDTH), BF16)],
        scratch_shapes=[pltpu.VMEM((SEQ // ATT_BK, DIFF_V_WIDTH, ATT_BK), BF16),
                        pltpu.VMEM((DIFF_MAPS, DIFF_V_DIM, ATT_TQ), F32),
                        pltpu.VMEM((DIFF_MAPS, 1, ATT_TQ), F32),
                        pltpu.VMEM((DIFF_MAPS, 1, ATT_TQ), F32),
                        pltpu.VMEM((DIFF_MAPS, ATT_BK, ATT_TQ), F32)],
        args=(lam_rows, subln_col, qk3, qk3, proj3),
        casts=casts)


def _out_proj_kernel(*refs, n_in, rider):
    a_refs = refs[:n_in]
    w_refs = refs[n_in:2 * n_in]
    g_ref, h_ref, o_ref = refs[2 * n_in:]
    rider()
    m = jnp.dot(a_refs[0][...], w_refs[0][...], preferred_element_type=F32)
    for a_ref, w_ref in zip(a_refs[1:], w_refs[1:]):
        m = m + jnp.dot(a_ref[...], w_ref[...], preferred_element_type=F32)
    o_ref[...] = m
    _add_normalized_rows(h_ref, o_ref, g_ref, o_ref)


def _out_proj(acts, w, g, h, name, casts=()):
    n_in = len(acts)
    k = w.shape[0] // n_in
    in_specs = ([pl.BlockSpec((OUT_TM, k), lambda i: (i, 0)) for _ in acts]
                + [pl.BlockSpec((k, D_MODEL), functools.partial(lambda i, r: (r, 0), r=r))
                   for r in range(n_in)]
                + [pl.BlockSpec((1, D_MODEL), lambda i: (0, 0)),
                   pl.BlockSpec((OUT_TM, D_MODEL), lambda i: (i, 0))])
    return _call(
        functools.partial(_out_proj_kernel, n_in=n_in),
        name=name,
        grid=(TOKENS // OUT_TM,),
        in_specs=in_specs,
        out_specs=[pl.BlockSpec((OUT_TM, D_MODEL), lambda i: (i, 0))],
        out_shape=[jax.ShapeDtypeStruct((TOKENS, D_MODEL), F32)],
        args=(*acts, *([w] * n_in), g, h),
        casts=casts)


def _ffn_kernel(h_ref, hnext_ref, g1_ref, w1_ref, w2_ref, g2_ref, o_ref, hn_even, hn_odd, *, rider):
    i = pl.program_id(0)
    f = pl.program_id(1)
    n_f = D_FF // FFN_TF
    rows_per_step = FFN_TM // n_f
    g1 = g1_ref[...]

    @pl.when(jnp.logical_and(i == 0, f == 0))
    def _():
        _normalize_rows(h_ref, g1_ref, hn_even)

    def step(cur_ref, nxt_ref, first):
        rider()
        for c in range(rows_per_step // NORM_ROWS):
            rows = pl.ds(pl.multiple_of(f * rows_per_step + c * NORM_ROWS, NORM_ROWS), NORM_ROWS)
            x = hnext_ref[rows, :]
            nxt_ref[rows, :] = (x * _rms_scale(x) * g1).astype(BF16)
        a = jnp.maximum(jnp.dot(cur_ref[...], w1_ref[...], preferred_element_type=F32), 0.0)
        part = jnp.dot((a * a).astype(BF16), w2_ref[...], preferred_element_type=F32)
        if first:
            o_ref[...] = part
        else:
            o_ref[...] += part

    even_tile = lax.rem(i, 2) == 0
    for is_even, cur_ref, nxt_ref in ((True, hn_even, hn_odd), (False, hn_odd, hn_even)):
        for first in (True, False):
            cond = jnp.logical_and(even_tile == is_even, (f == 0) == first)
            pl.when(cond)(functools.partial(step, cur_ref, nxt_ref, first))

    @pl.when(f == n_f - 1)
    def _():
        _add_normalized_rows(h_ref, o_ref, g2_ref, o_ref)


def _ffn(h, g1, w1, w2, g2, casts):
    n_tiles = TOKENS // FFN_TM
    return _call(
        _ffn_kernel,
        name="ffn",
        grid=(n_tiles, D_FF // FFN_TF),
        in_specs=[pl.BlockSpec((FFN_TM, D_MODEL), lambda i, f: (i, 0)),
                  pl.BlockSpec((FFN_TM, D_MODEL), lambda i, f: (jnp.minimum(i + 1, n_tiles - 1), 0)),
                  pl.BlockSpec((1, D_MODEL), lambda i, f: (0, 0)),
                  pl.BlockSpec((None, D_MODEL, FFN_TF), lambda i, f: (f, 0, 0)),
                  pl.BlockSpec((FFN_TF, D_MODEL), lambda i, f: (f, 0)),
                  pl.BlockSpec((1, D_MODEL), lambda i, f: (0, 0))],
        out_specs=[pl.BlockSpec((FFN_TM, D_MODEL), lambda i, f: (i, 0))],
        out_shape=[jax.ShapeDtypeStruct((TOKENS, D_MODEL), F32)],
        scratch_shapes=[pltpu.VMEM((FFN_TM, D_MODEL), BF16), pltpu.VMEM((FFN_TM, D_MODEL), BF16)],
        args=(h, h, g1, w1, w2, g2),
        casts=casts)


def _ple_kernel(h_ref, p_ref, wg_ref, wp_ref, g_ref, o_ref, *, rider):
    rider()
    z = jnp.dot(h_ref[...].astype(BF16), wg_ref[...], preferred_element_type=F32)
    e = jnp.dot(p_ref[...].astype(BF16), wp_ref[...], preferred_element_type=F32)
    o_ref[...] = e / (1.0 + jnp.exp2(z * (-LOG2E)))
    _add_normalized_rows(h_ref, o_ref, g_ref, o_ref)


def _ple(h, p_stack, layer, wg, wp, g, casts=()):
    return _call(
        _ple_kernel,
        name="ple",
        grid=(TOKENS // PLE_TM,),
        in_specs=[pl.BlockSpec((PLE_TM, D_MODEL), lambda i: (i, 0)),
                  pl.BlockSpec((None, PLE_TM, PLE_DIM), lambda i: (layer, i, 0)),
                  pl.BlockSpec((D_MODEL, D_MODEL), lambda i: (0, 0)),
                  pl.BlockSpec((PLE_DIM, D_MODEL), lambda i: (0, 0)),
                  pl.BlockSpec((1, D_MODEL), lambda i: (0, 0))],
        out_specs=[pl.BlockSpec((PLE_TM, D_MODEL), lambda i: (i, 0))],
        out_shape=[jax.ShapeDtypeStruct((TOKENS, D_MODEL), F32)],
        args=(h, p_stack, wg, wp, g),
        casts=casts)


def _sg_kernel(u_ref, v_ref, lng_ref, lnb_ref, ws_ref, bst_ref, y_ref, vn_ref, *, rider):
    rider()
    v = v_ref[...].astype(F32)
    mu = jnp.mean(v, axis=-1, keepdims=True)
    vc = v - mu
    inv = lax.rsqrt(jnp.mean(vc * vc, axis=-1, keepdims=True) + EPS)
    vn_ref[...] = (vc * inv * lng_ref[...] + lnb_ref[...]).astype(BF16)

    n_blk = SG_TM // SG_BLOCK
    t_idx = lax.broadcasted_iota(jnp.int32, (SG_BLOCK, SG_BLOCK), 0)
    s_idx = lax.broadcasted_iota(jnp.int32, (SG_BLOCK, SG_BLOCK), 1)
    visible = _chunk_of(s_idx) <= _chunk_of(t_idx)
    bst = bst_ref[...]
    for g in range(SG_GROUPS):
        cols = slice(g * SG_GROUP_DIM, (g + 1) * SG_GROUP_DIM)
        w = jnp.where(visible, ws_ref[g], 0.0).astype(BF16)
        rhs = jnp.concatenate(
            [vn_ref[n * SG_BLOCK:(n + 1) * SG_BLOCK, cols] for n in range(n_blk)], axis=1)
        mixed = jnp.dot(w, rhs, preferred_element_type=F32) + bst[:, g:g + 1]
        for n in range(n_blk):
            rows = slice(n * SG_BLOCK, (n + 1) * SG_BLOCK)
            u = u_ref[rows, cols].astype(F32)
            y_ref[rows, cols] = (u * mixed[:, n * SG_BLOCK:(n + 1) * SG_BLOCK]).astype(y_ref.dtype)


def _spatial_gate(uv, ln_g, ln_b, w_s_stack, layer, b_s_t, casts):
    return _call(
        _sg_kernel,
        name="spatial_gate",
        grid=(TOKENS // SG_TM,),
        in_specs=[pl.BlockSpec((SG_TM, SG_WIDTH), lambda i: (i, 0)),
                  pl.BlockSpec((SG_TM, SG_WIDTH), lambda i: (i, 1)),
                  pl.BlockSpec((1, SG_WIDTH), lambda i: (0, 0)),
                  pl.BlockSpec((1, SG_WIDTH), lambda i: (0, 0)),
                  pl.BlockSpec((None, SG_GROUPS, SG_BLOCK, SG_BLOCK), lambda i: (layer, 0, 0, 0)),
                  pl.BlockSpec((SG_BLOCK, SG_GROUPS), lambda i: (0, 0))],
        out_specs=[pl.BlockSpec((SG_TM, SG_WIDTH), lambda i: (i, 0))],
        out_shape=[jax.ShapeDtypeStruct((TOKENS, SG_WIDTH), BF16)],
        scratch_shapes=[pltpu.VMEM((SG_TM, SG_WIDTH), BF16)],
        args=(uv, uv, ln_g, ln_b, w_s_stack, b_s_t),
        casts=casts)


def _row(v):
    return v.reshape(1, -1)


def kernel(x, p, positions, ev_norm_pre, ev_w_in, ev_lam_q1, ev_lam_k1, ev_lam_q2, ev_lam_k2,
           ev_subln, ev_w_out, ev_norm_post, od_norm_pre, od_w_in, od_ln_g, od_ln_b, od_w_s,
           od_b_s, od_w_out, od_norm_post, ffn_norm_pre, ffn_w1, ffn_w2, ffn_norm_post,
           ple_w_proj, ple_w_gate, ple_norm):
    assert DEPTH == 2, "the cast schedule below is written for one even and one odd layer"
    h = x.reshape(TOKENS, D_MODEL)
    p_stack = p.reshape(DEPTH, TOKENS, PLE_DIM)
    cos_t, sin_t = _rope_tables(positions)
    sb_q_scale = jnp.concatenate([jnp.full((1, SB_WIDTH), Q_SCALE, F32),
                                  jnp.ones((1, (len(PLAIN_W_TILES) - 1) * PROJ_TN), F32)], axis=1)
    w_in0 = ev_w_in[0].astype(BF16).reshape(D_MODEL, -1, PROJ_TN).transpose(1, 0, 2)

    lambda_init = 0.8 - 0.6 * math.exp(-0.3 * 0)
    (proj, xn), (w_out0,) = _plain_proj(h, _row(ev_norm_pre[0]), w_in0, sb_q_scale,
                                        casts=[_Cast(ev_w_out, 0)])
    (qk,), (ffn_w2_0,) = _rope_proj(xn, w_in0, cos_t, sin_t, casts=[_Cast(ffn_w2, 0)])
    proj3 = proj.reshape(BATCH, SEQ, len(PLAIN_W_TILES) * PROJ_TN)
    qk3 = qk.reshape(BATCH, SEQ, 2 * DIFF_QK_WIDTH)
    (sb_o,), (ffn_w1_0,) = _sb_attention(proj3, casts=[_Cast(ffn_w1, 0, FFN_TF)])
    lam_rows = jnp.stack([ev_lam_q1[0], ev_lam_k1[0], ev_lam_q2[0], ev_lam_k2[0]])
    (df_o,), (gate_0, pproj_0) = _diff_attention(
        qk3, proj3, lam_rows, ev_subln[0].reshape(DIFF_V_DIM, 1), lambda_init,
        casts=[_Cast(ple_w_gate, 0), _Cast(ple_w_proj, 0)])
    (h,), _ = _out_proj([sb_o.reshape(TOKENS, SB_WIDTH), df_o.reshape(TOKENS, DIFF_V_WIDTH)],
                        w_out0, _row(ev_norm_post[0]), h, "even_out_proj")
    (h,), _ = _ffn(h, _row(ffn_norm_pre[0]), ffn_w1_0, ffn_w2_0, _row(ffn_norm_post[0]), casts=[])
    (h,), (od_in, od_out) = _ple(h, p_stack, 0, gate_0, pproj_0, _row(ple_norm[0]),
                                 casts=[_Cast(od_w_in, 0, PROJ_TN), _Cast(od_w_out, 0)])

    (uv,), (ffn_w1_1,) = _odd_proj(h, _row(od_norm_pre[0]), od_in, casts=[_Cast(ffn_w1, 1, FFN_TF)])
    (y,), (gate_1, pproj_1) = _spatial_gate(
        uv, _row(od_ln_g[0]), _row(od_ln_b[0]), od_w_s, 0, od_b_s[0].T,
        casts=[_Cast(ple_w_gate, 1), _Cast(ple_w_proj, 1)])
    (h,), (ffn_w2_1,) = _out_proj([y], od_out, _row(od_norm_post[0]), h, "odd_out_proj",
                                  casts=[_Cast(ffn_w2, 1)])
    (h,), _ = _ffn(h, _row(ffn_norm_pre[1]), ffn_w1_1, ffn_w2_1, _row(ffn_norm_post[1]), casts=[])
    (h,), _ = _ple(h, p_stack, 1, gate_1, pproj_1, _row(ple_norm[1]))
    return h.reshape(BATCH, SEQ, D_MODEL)
```

```python
import functools
import math
from typing import NamedTuple

import jax
import jax.numpy as jnp
from jax import lax
from jax.experimental import pallas as pl
from jax.experimental.pallas import tpu as pltpu

D_MODEL = 2048
BATCH = 8
SEQ = 2048
DEPTH = 2
TOKENS = BATCH * SEQ

CHUNK = 64
HEAD_DIM = 128
SB_HEADS = 8
DIFF_HEADS = 4
DIFF_MAPS = 2 * DIFF_HEADS
DIFF_V_DIM = 2 * HEAD_DIM
ROT_DIM = HEAD_DIM // 4
ROT_HALF = ROT_DIM // 2
ROPE_PACK = 128 // ROT_DIM
ROPE_THETA = 500000.0
SG_BLOCK = 128
SG_GROUPS = 16
SG_GROUP_DIM = 128
SG_WIDTH = SG_GROUPS * SG_GROUP_DIM
D_FF = 4 * D_MODEL
PLE_DIM = 256
SB_WIDTH = SB_HEADS * HEAD_DIM
DIFF_QK_WIDTH = DIFF_HEADS * 2 * HEAD_DIM
DIFF_V_WIDTH = DIFF_HEADS * DIFF_V_DIM
EVEN_IN_WIDTH = 3 * SB_WIDTH + 2 * DIFF_QK_WIDTH + DIFF_V_WIDTH
EPS = 1e-6
LOG2E = math.log2(math.e)
Q_SCALE = HEAD_DIM ** -0.5 * LOG2E
EXP2_ZERO_BELOW = -151.0

LANES = 128
BF16_SUBLANES = 16
VMEM_LIMIT_BYTES = 56 * 1024 * 1024

F32 = jnp.float32
BF16 = jnp.bfloat16

PROJ_TM = 1024
PROJ_TN = 1024
ATT_TQ = 256
ATT_BK = 256
SB_NH = 8
OUT_TM = 512
FFN_TM = 512
FFN_TF = 1024
PLE_TM = 512
SG_TM = 512
ROPE_TM = 2048
NORM_ROWS = 16

PLAIN_W_TILES = (0, 1, 2, 5)
ROPE_W_TILE0 = 3


def _chunk_of(idx):
    return jnp.right_shift(idx, CHUNK.bit_length() - 1)


def _rms_scale(x):
    return lax.rsqrt(jnp.mean(x * x, axis=-1, keepdims=True) + EPS)


class _Cast(NamedTuple):
    stack: jax.Array
    layer: int


def _no_rider():
    pass


def _call(body, *, name, grid, in_specs, out_specs, out_shape, args, scratch_shapes=(), casts=()):
    steps = math.prod(grid)
    n_in, n_out, n_cast = len(in_specs), len(out_specs), len(casts)

    def linear_step(*idx):
        step = idx[0]
        for extent, k in zip(grid[1:], idx[1:]):
            step = step * extent + k
        return step

    cast_in, cast_out, cast_shape = [], [], []
    for job in casts:
        _, rows, cols = job.stack.shape
        assert rows % BF16_SUBLANES == 0
        n_blocks = math.gcd(steps, rows // BF16_SUBLANES)
        blk_rows = rows // n_blocks
        repeat = steps // n_blocks
        cast_in.append(pl.BlockSpec(
            (None, blk_rows, cols),
            functools.partial(lambda *idx, layer, repeat: (layer, linear_step(*idx) // repeat, 0),
                              layer=job.layer, repeat=repeat)))
        cast_out.append(pl.BlockSpec(
            (blk_rows, cols),
            functools.partial(lambda *idx, repeat: (linear_step(*idx) // repeat, 0), repeat=repeat)))
        cast_shape.append(jax.ShapeDtypeStruct((rows, cols), BF16))

    def wrapped(*refs):
        ins = refs[:n_in]
        srcs = refs[n_in:n_in + n_cast]
        outs = refs[n_in + n_cast:n_in + n_cast + n_out]
        dsts = refs[n_in + n_cast + n_out:n_in + 2 * n_cast + n_out]
        scratch = refs[n_in + 2 * n_cast + n_out:]

        def rider():
            for src, dst in zip(srcs, dsts):
                dst[...] = src[...].astype(dst.dtype)

        body(*ins, *outs, *scratch, rider=rider if n_cast else _no_rider)

    results = pl.pallas_call(
        wrapped,
        grid=grid,
        in_specs=list(in_specs) + cast_in,
        out_specs=list(out_specs) + cast_out,
        out_shape=list(out_shape) + cast_shape,
        scratch_shapes=list(scratch_shapes),
        compiler_params=pltpu.CompilerParams(dimension_semantics=("arbitrary",) * len(grid),
                                             vmem_limit_bytes=VMEM_LIMIT_BYTES),
        name=name,
    )(*args, *[job.stack for job in casts])
    return results[:n_out], results[n_out:]


def _rope_table_kernel(pos_ref, freq_ref, cos_ref, sin_ref, *, rider):
    rider()
    ang = pos_ref[...].astype(F32) * freq_ref[...]
    lane = lax.broadcasted_iota(jnp.int32, ang.shape, 1)
    c = jnp.cos(ang)
    s = jnp.sin(ang)
    s = jnp.where(jnp.bitwise_and(lane, ROT_DIM - 1) < ROT_HALF, -s, s)
    rotated = lane < ROT_DIM
    n_rows = ang.shape[0]
    for k in range(ROPE_PACK):
        shift = (LANES - k * ROT_DIM) % LANES
        ck = jnp.where(rotated, pltpu.roll(c, shift, 1) if shift else c, 1.0)
        sk = jnp.where(rotated, pltpu.roll(s, shift, 1) if shift else s, 0.0)
        rows = pl.ds(k, n_rows, stride=ROPE_PACK)
        cos_ref[0, rows, :] = ck * Q_SCALE
        sin_ref[0, rows, :] = sk * Q_SCALE
        cos_ref[1, rows, :] = ck
        sin_ref[1, rows, :] = sk


def _rope_tables(positions):
    inv_freq = ROPE_THETA ** (-jnp.arange(0, ROT_DIM, 2, dtype=F32) / ROT_DIM)
    freq_row = jnp.tile(jnp.concatenate([inv_freq, inv_freq]), ROPE_PACK).reshape(1, LANES)
    pos = jnp.repeat(positions.reshape(TOKENS // ROPE_PACK, ROPE_PACK), ROT_DIM, axis=1)
    (cos_t, sin_t), _ = _call(
        _rope_table_kernel,
        name="rope_tables",
        grid=(TOKENS // ROPE_TM,),
        in_specs=[pl.BlockSpec((ROPE_TM // ROPE_PACK, LANES), lambda i: (i, 0)),
                  pl.BlockSpec((1, LANES), lambda i: (0, 0))],
        out_specs=[pl.BlockSpec((2, ROPE_TM, LANES), lambda i: (0, i, 0)),
                   pl.BlockSpec((2, ROPE_TM, LANES), lambda i: (0, i, 0))],
        out_shape=[jax.ShapeDtypeStruct((2, TOKENS, LANES), F32)] * 2,
        args=(pos, freq_row))
    return cos_t, sin_t


def _normalize_rows(x_ref, g_ref, xn_ref):
    g = g_ref[...]
    for r0 in range(0, x_ref.shape[0], NORM_ROWS):
        x = x_ref[r0:r0 + NORM_ROWS, :]
        xn_ref[r0:r0 + NORM_ROWS, :] = (x * _rms_scale(x) * g).astype(BF16)


def _add_normalized_rows(h_ref, m_ref, g_ref, o_ref):
    g = g_ref[...]
    for r0 in range(0, h_ref.shape[0], NORM_ROWS):
        rows = slice(r0, r0 + NORM_ROWS)
        m = m_ref[rows, :]
        o_ref[rows, :] = h_ref[rows, :] + m * _rms_scale(m) * g


def _plain_proj_kernel(x_ref, g_ref, w_ref, cs_ref, o_ref, xn_ref, *, rider):
    @pl.when(pl.program_id(1) == 0)
    def _():
        _normalize_rows(x_ref, g_ref, xn_ref)

    rider()
    acc = jnp.dot(xn_ref[...], w_ref[...], preferred_element_type=F32)
    o_ref[...] = (acc * cs_ref[...]).astype(o_ref.dtype)


def _plain_proj(h, g, w, col_scale, casts):
    n_tiles = len(PLAIN_W_TILES)
    first_gap = PLAIN_W_TILES.index(5)

    def w_map(i, j):
        return (0, jnp.where(j < first_gap, j, j + (5 - first_gap)))

    return _call(
        _plain_proj_kernel,
        name="even_plain_proj",
        grid=(TOKENS // PROJ_TM, n_tiles),
        in_specs=[pl.BlockSpec((PROJ_TM, D_MODEL), lambda i, j: (i, 0)),
                  pl.BlockSpec((1, D_MODEL), lambda i, j: (0, 0)),
                  pl.BlockSpec((D_MODEL, PROJ_TN), w_map),
                  pl.BlockSpec((1, PROJ_TN), lambda i, j: (0, j))],
        out_specs=[pl.BlockSpec((PROJ_TM, PROJ_TN), lambda i, j: (i, j)),
                   pl.BlockSpec((PROJ_TM, D_MODEL), lambda i, j: (i, 0))],
        out_shape=[jax.ShapeDtypeStruct((TOKENS, n_tiles * PROJ_TN), BF16),
                   jax.ShapeDtypeStruct((TOKENS, D_MODEL), BF16)],
        args=(h, g, w, col_scale),
        casts=casts)


def _rope_proj_kernel(xn_ref, w_ref, cos_ref, sin_ref, o_ref, *, rider):
    rider()
    acc = jnp.dot(xn_ref[...], w_ref[...], preferred_element_type=F32)
    c = cos_ref[...]
    s = sin_ref[...]
    first = lax.broadcasted_iota(jnp.int32, c.shape, 1) < ROT_HALF
    for g in range(PROJ_TN // LANES):
        a = acc[:, g * LANES:(g + 1) * LANES]
        swapped = jnp.where(first, pltpu.roll(a, LANES - ROT_HALF, 1), pltpu.roll(a, ROT_HALF, 1))
        o_ref[:, g * LANES:(g + 1) * LANES] = (a * c + swapped * s).astype(o_ref.dtype)


def _rope_proj(xn, w, cos_t, sin_t, casts):
    return _call(
        _rope_proj_kernel,
        name="even_rope_proj",
        grid=(TOKENS // PROJ_TM, 2),
        in_specs=[pl.BlockSpec((PROJ_TM, D_MODEL), lambda i, j: (i, 0)),
                  pl.BlockSpec((D_MODEL, PROJ_TN), lambda i, j: (0, ROPE_W_TILE0 + j)),
                  pl.BlockSpec((None, PROJ_TM, LANES), lambda i, j: (j, i, 0)),
                  pl.BlockSpec((None, PROJ_TM, LANES), lambda i, j: (j, i, 0))],
        out_specs=[pl.BlockSpec((PROJ_TM, PROJ_TN), lambda i, j: (i, j))],
        out_shape=[jax.ShapeDtypeStruct((TOKENS, 2 * PROJ_TN), BF16)],
        args=(xn, w, cos_t, sin_t),
        casts=casts)


def _gelu_tanh(x):
    c = -2.0 * math.sqrt(2.0 / math.pi) * LOG2E
    return x / (1.0 + jnp.exp2(x * (c + (0.044715 * c) * (x * x))))


def _odd_proj_kernel(x_ref, g_ref, w_ref, o_ref, xn_ref, *, rider):
    @pl.when(pl.program_id(1) == 0)
    def _():
        _normalize_rows(x_ref, g_ref, xn_ref)

    rider()
    acc = jnp.dot(xn_ref[...], w_ref[...], preferred_element_type=F32)
    o_ref[...] = _gelu_tanh(acc).astype(o_ref.dtype)


def _odd_proj(h, g, w, casts):
    n = w.shape[1]
    return _call(
        _odd_proj_kernel,
        name="odd_in_proj",
        grid=(TOKENS // PROJ_TM, n // PROJ_TN),
        in_specs=[pl.BlockSpec((PROJ_TM, D_MODEL), lambda i, j: (i, 0)),
                  pl.BlockSpec((1, D_MODEL), lambda i, j: (0, 0)),
                  pl.BlockSpec((D_MODEL, PROJ_TN), lambda i, j: (0, j))],
        out_specs=[pl.BlockSpec((PROJ_TM, PROJ_TN), lambda i, j: (i, j))],
        out_shape=[jax.ShapeDtypeStruct((TOKENS, n), BF16)],
        scratch_shapes=[pltpu.VMEM((PROJ_TM, D_MODEL), BF16)],
        args=(h, g, w),
        casts=casts)


def _store_transposed_values(v_ref, vt_ref):
    for jb in range(SEQ // ATT_BK):
        vt_ref[jb] = v_ref[jb * ATT_BK:(jb + 1) * ATT_BK, :].astype(F32).T.astype(BF16)


def _scores_t(k_ref, q_ref, j, cols):
    kj = k_ref[pl.ds(j * ATT_BK, ATT_BK), cols]
    return lax.dot_general(kj, q_ref[:, cols], (((1,), (1,)), ((), ())),
                           preferred_element_type=F32)


def _sb_attn_kernel(q_ref, k_ref, v_ref, o_ref, vt_ref, acc_ref, carry_ref, *, rider):
    i = pl.program_id(2)

    @pl.when(i == 0)
    def _():
        _store_transposed_values(v_ref, vt_ref)

    rider()
    key = lax.broadcasted_iota(jnp.int32, (ATT_BK, ATT_TQ), 0)
    qry = lax.broadcasted_iota(jnp.int32, (ATT_BK, ATT_TQ), 1)
    causal = key < qry
    later_key = lax.broadcasted_iota(jnp.int32, (ATT_BK, ATT_BK), 1)
    this_key = lax.broadcasted_iota(jnp.int32, (ATT_BK, ATT_BK), 0)
    tri = (later_key > this_key).astype(BF16)

    heads = range(SB_NH)

    def head_cols(h):
        return slice(h * HEAD_DIM, (h + 1) * HEAD_DIM)

    def blocks(j, carries, diag):
        zs = [_scores_t(k_ref, q_ref, j, head_cols(h)) for h in heads]
        log_betas, log_keeps = [], []
        for z in zs:
            soft = jnp.log2(1.0 + jnp.exp2(-jnp.abs(z)))
            log_beta = jnp.minimum(z, 0.0) - soft
            log_keep = log_beta - z
            if diag:
                log_keep = jnp.where(causal, log_keep, 0.0)
            log_betas.append(log_beta)
            log_keeps.append(log_keep)
        laters = [jnp.dot(tri, lk.astype(BF16), preferred_element_type=F32) for lk in log_keeps]
        ws = []
        for h in heads:
            arg = log_betas[h] + laters[h]
            if carries is not None:
                arg = arg + carries[h]
            w = jnp.exp2(arg)
            if diag:
                w = jnp.where(causal, w, 0.0)
            ws.append(w.astype(BF16))
        contribs = [jnp.dot(vt_ref[j, head_cols(h), :], ws[h], preferred_element_type=F32)
                    for h in heads]
        colsums = [laters[h][0:1, :] + log_keeps[h][0:1, :] for h in heads]
        return contribs, colsums

    def any_weight_left(carries):
        worst = functools.reduce(jnp.maximum, carries)
        return (jnp.max(worst) > EXP2_ZERO_BELOW).astype(jnp.int32)

    contribs, colsums = blocks(i, None, True)
    for h in heads:
        acc_ref[h] = contribs[h]
        carry_ref[h] = colsums[h]

    def cond(state):
        j, more = state
        return jnp.logical_and(j >= 0, more > 0)

    def body(state):
        j, _ = state
        carries = [carry_ref[h] for h in heads]
        contribs, colsums = blocks(j, carries, False)
        carries = [carries[h] + colsums[h] for h in heads]
        for h in heads:
            acc_ref[h] += contribs[h]
            carry_ref[h] = carries[h]
        return j - 1, any_weight_left(carries)

    lax.while_loop(cond, body, (i - 1, any_weight_left(colsums)))
    for h in range(SB_NH):
        o_ref[:, h * HEAD_DIM:(h + 1) * HEAD_DIM] = acc_ref[h].T.astype(o_ref.dtype)


def _sb_attention(proj3, casts):
    width = SB_NH * HEAD_DIM
    groups = SB_HEADS // SB_NH
    return _call(
        _sb_attn_kernel,
        name="sb_attention",
        grid=(BATCH, groups, SEQ // ATT_TQ),
        in_specs=[pl.BlockSpec((None, ATT_TQ, width), lambda b, g, i: (b, i, g)),
                  pl.BlockSpec((None, SEQ, width), lambda b, g, i: (b, 0, groups + g)),
                  pl.BlockSpec((None, SEQ, width), lambda b, g, i: (b, 0, 2 * groups + g))],
        out_specs=[pl.BlockSpec((None, ATT_TQ, width), lambda b, g, i: (b, i, g))],
        out_shape=[jax.ShapeDtypeStruct((BATCH, SEQ, SB_WIDTH), BF16)],
        scratch_shapes=[pltpu.VMEM((SEQ // ATT_BK, width, ATT_BK), BF16),
                        pltpu.VMEM((SB_NH, HEAD_DIM, ATT_TQ), F32),
                        pltpu.VMEM((SB_NH, 1, ATT_TQ), F32)],
        args=(proj3, proj3, proj3),
        casts=casts)


def _diff_attn_kernel(lam_ref, g_ref, q_ref, k_ref, v_ref, o_ref, vt_ref, acc_ref, m_ref, l_ref,
                      z_ref, *, lambda_init, rider):
    i = pl.program_id(1)

    @pl.when(i == 0)
    def _():
        _store_transposed_values(v_ref, vt_ref)

    rider()
    key = lax.broadcasted_iota(jnp.int32, (ATT_BK, ATT_TQ), 0)
    qry = lax.broadcasted_iota(jnp.int32, (ATT_BK, ATT_TQ), 1)
    visible = _chunk_of(key) <= _chunk_of(qry)

    def head_cols(c):
        return slice(c * HEAD_DIM, (c + 1) * HEAD_DIM)

    def value_rows(c):
        return slice((c // 2) * DIFF_V_DIM, (c // 2 + 1) * DIFF_V_DIM)

    maps = range(DIFF_MAPS)

    def scores_all(j):
        return [_scores_t(k_ref, q_ref, j, head_cols(c)) for c in maps]

    def absorb(j):
        ps, alphas = [], []
        for c in maps:
            z = z_ref[c]
            m_old = m_ref[c]
            m_new = jnp.maximum(m_old, jnp.max(z, axis=0, keepdims=True))
            alpha = jnp.exp2(m_old - m_new)
            p = jnp.exp2(z - m_new)
            m_ref[c] = m_new
            l_ref[c] = l_ref[c] * alpha + jnp.sum(p, axis=0, keepdims=True)
            ps.append(p.astype(BF16))
            alphas.append(alpha)
        pvs = [jnp.dot(vt_ref[j, value_rows(c), :], ps[c], preferred_element_type=F32)
               for c in maps]
        for c in maps:
            acc_ref[c] = acc_ref[c] * alphas[c] + pvs[c]

    for c in maps:
        m_ref[c] = jnp.full((1, ATT_TQ), -jnp.inf, F32)
        l_ref[c] = jnp.zeros((1, ATT_TQ), F32)
        acc_ref[c] = jnp.zeros((DIFF_V_DIM, ATT_TQ), F32)
    for c, z in enumerate(scores_all(i)):
        z_ref[c] = jnp.where(visible, z, -jnp.inf)

    def body(jj, unused):
        j = i - jj
        zs_next = scores_all(j - 1)
        absorb(j)
        for c in maps:
            z_ref[c] = zs_next[c]
        return unused

    lax.fori_loop(0, i, body, 0)
    absorb(0)

    lam_rows = lam_ref[...]
    s1 = jnp.sum(lam_rows[0:1, :] * lam_rows[1:2, :], axis=-1, keepdims=True)
    s2 = jnp.sum(lam_rows[2:3, :] * lam_rows[3:4, :], axis=-1, keepdims=True)
    lam = jnp.exp(s1) - jnp.exp(s2) + lambda_init
    gain = g_ref[...]
    for h in range(DIFF_HEADS):
        o = acc_ref[2 * h] / l_ref[2 * h] - lam * (acc_ref[2 * h + 1] / l_ref[2 * h + 1])
        r = lax.rsqrt(jnp.mean(o * o, axis=0, keepdims=True) + EPS)
        o = o * r * gain * (1.0 - lambda_init)
        o_ref[:, h * DIFF_V_DIM:(h + 1) * DIFF_V_DIM] = o.T.astype(o_ref.dtype)


def _diff_attention(qk3, proj3, lam_rows, subln_col, lambda_init, casts):
    v_tile = PLAIN_W_TILES.index(5)
    return _call(
        functools.partial(_diff_attn_kernel, lambda_init=lambda_init),
        name="diff_attention",
        grid=(BATCH, SEQ // ATT_TQ),
        in_specs=[pl.BlockSpec((4, HEAD_DIM), lambda b, i: (0, 0)),
                  pl.BlockSpec((DIFF_V_DIM, 1), lambda b, i: (0, 0)),
                  pl.BlockSpec((None, ATT_TQ, DIFF_QK_WIDTH), lambda b, i: (b, i, 0)),
                  pl.BlockSpec((None, SEQ, DIFF_QK_WIDTH), lambda b, i: (b, 0, 1)),
                  pl.BlockSpec((None, SEQ, DIFF_V_WIDTH), lambda b, i: (b, 0, v_tile))],
        out_specs=[pl.BlockSpec((None, ATT_TQ, DIFF_V_WIDTH), lambda b, i: (b, i, 0))],
        out_shape=[jax.ShapeDtypeStruct((BATCH, SEQ, DIFF_V_WIDTH), BF16)],
        scratch_shapes=[pltpu.VMEM((SEQ // ATT_BK, DIFF_V_WIDTH, ATT_BK), BF16),
                        pltpu.VMEM((DIFF_MAPS, DIFF_V_DIM, ATT_TQ), F32),
                        pltpu.VMEM((DIFF_MAPS, 1, ATT_TQ), F32),
                        pltpu.VMEM((DIFF_MAPS, 1, ATT_TQ), F32),
                        pltpu.VMEM((DIFF_MAPS, ATT_BK, ATT_TQ), F32)],
        args=(lam_rows, subln_col, qk3, qk3, proj3),
        casts=casts)


def _out_proj_kernel(*refs, n_in, rider):
    a_refs = refs[:n_in]
    w_refs = refs[n_in:2 * n_in]
    g_ref, h_ref, o_ref = refs[2 * n_in:]
    rider()
    m = jnp.dot(a_refs[0][...], w_refs[0][...], preferred_element_type=F32)
    for a_ref, w_ref in zip(a_refs[1:], w_refs[1:]):
        m = m + jnp.dot(a_ref[...], w_ref[...], preferred_element_type=F32)
    o_ref[...] = m
    _add_normalized_rows(h_ref, o_ref, g_ref, o_ref)


def _out_proj(acts, w, g, h, name, casts=()):
    n_in = len(acts)
    k = w.shape[0] // n_in
    in_specs = ([pl.BlockSpec((OUT_TM, k), lambda i: (i, 0)) for _ in acts]
                + [pl.BlockSpec((k, D_MODEL), functools.partial(lambda i, r: (r, 0), r=r))
                   for r in range(n_in)]
                + [pl.BlockSpec((1, D_MODEL), lambda i: (0, 0)),
                   pl.BlockSpec((OUT_TM, D_MODEL), lambda i: (i, 0))])
    return _call(
        functools.partial(_out_proj_kernel, n_in=n_in),
        name=name,
        grid=(TOKENS // OUT_TM,),
        in_specs=in_specs,
        out_specs=[pl.BlockSpec((OUT_TM, D_MODEL), lambda i: (i, 0))],
        out_shape=[jax.ShapeDtypeStruct((TOKENS, D_MODEL), F32)],
        args=(*acts, *([w] * n_in), g, h),
        casts=casts)


def _ffn_kernel(h_ref, g1_ref, w1_ref, w2_ref, g2_ref, o_ref, hn_ref, *, rider):
    f = pl.program_id(1)

    @pl.when(f == 0)
    def _():
        _normalize_rows(h_ref, g1_ref, hn_ref)
        o_ref[...] = jnp.zeros_like(o_ref)

    rider()
    a = jnp.maximum(jnp.dot(hn_ref[...], w1_ref[...], preferred_element_type=F32), 0.0)
    o_ref[...] += jnp.dot((a * a).astype(BF16), w2_ref[...], preferred_element_type=F32)

    @pl.when(f == pl.num_programs(1) - 1)
    def _():
        _add_normalized_rows(h_ref, o_ref, g2_ref, o_ref)


def _ffn(h, g1, w1, w2, g2, casts):
    return _call(
        _ffn_kernel,
        name="ffn",
        grid=(TOKENS // FFN_TM, D_FF // FFN_TF),
        in_specs=[pl.BlockSpec((FFN_TM, D_MODEL), lambda i, f: (i, 0)),
                  pl.BlockSpec((1, D_MODEL), lambda i, f: (0, 0)),
                  pl.BlockSpec((D_MODEL, FFN_TF), lambda i, f: (0, f)),
                  pl.BlockSpec((FFN_TF, D_MODEL), lambda i, f: (f, 0)),
                  pl.BlockSpec((1, D_MODEL), lambda i, f: (0, 0))],
        out_specs=[pl.BlockSpec((FFN_TM, D_MODEL), lambda i, f: (i, 0))],
        out_shape=[jax.ShapeDtypeStruct((TOKENS, D_MODEL), F32)],
        scratch_shapes=[pltpu.VMEM((FFN_TM, D_MODEL), BF16)],
        args=(h, g1, w1, w2, g2),
        casts=casts)


def _ple_kernel(h_ref, p_ref, wg_ref, wp_ref, g_ref, o_ref, *, rider):
    rider()
    z = jnp.dot(h_ref[...].astype(BF16), wg_ref[...], preferred_element_type=F32)
    e = jnp.dot(p_ref[...].astype(BF16), wp_ref[...], preferred_element_type=F32)
    o_ref[...] = e / (1.0 + jnp.exp2(z * (-LOG2E)))
    _add_normalized_rows(h_ref, o_ref, g_ref, o_ref)


def _ple(h, p_stack, layer, wg, wp, g, casts=()):
    return _call(
        _ple_kernel,
        name="ple",
        grid=(TOKENS // PLE_TM,),
        in_specs=[pl.BlockSpec((PLE_TM, D_MODEL), lambda i: (i, 0)),
                  pl.BlockSpec((None, PLE_TM, PLE_DIM), lambda i: (layer, i, 0)),
                  pl.BlockSpec((D_MODEL, D_MODEL), lambda i: (0, 0)),
                  pl.BlockSpec((PLE_DIM, D_MODEL), lambda i: (0, 0)),
                  pl.BlockSpec((1, D_MODEL), lambda i: (0, 0))],
        out_specs=[pl.BlockSpec((PLE_TM, D_MODEL), lambda i: (i, 0))],
        out_shape=[jax.ShapeDtypeStruct((TOKENS, D_MODEL), F32)],
        args=(h, p_stack, wg, wp, g),
        casts=casts)


def _sg_kernel(u_ref, v_ref, lng_ref, lnb_ref, ws_ref, bst_ref, y_ref, vn_ref, *, rider):
    rider()
    v = v_ref[...].astype(F32)
    mu = jnp.mean(v, axis=-1, keepdims=True)
    vc = v - mu
    inv = lax.rsqrt(jnp.mean(vc * vc, axis=-1, keepdims=True) + EPS)
    vn_ref[...] = (vc * inv * lng_ref[...] + lnb_ref[...]).astype(BF16)

    n_blk = SG_TM // SG_BLOCK
    t_idx = lax.broadcasted_iota(jnp.int32, (SG_BLOCK, SG_BLOCK), 0)
    s_idx = lax.broadcasted_iota(jnp.int32, (SG_BLOCK, SG_BLOCK), 1)
    visible = _chunk_of(s_idx) <= _chunk_of(t_idx)
    bst = bst_ref[...]
    for g in range(SG_GROUPS):
        cols = slice(g * SG_GROUP_DIM, (g + 1) * SG_GROUP_DIM)
        w = jnp.where(visible, ws_ref[g], 0.0).astype(BF16)
        rhs = jnp.concatenate(
            [vn_ref[n * SG_BLOCK:(n + 1) * SG_BLOCK, cols] for n in range(n_blk)], axis=1)
        mixed = jnp.dot(w, rhs, preferred_element_type=F32) + bst[:, g:g + 1]
        for n in range(n_blk):
            rows = slice(n * SG_BLOCK, (n + 1) * SG_BLOCK)
            u = u_ref[rows, cols].astype(F32)
            y_ref[rows, cols] = (u * mixed[:, n * SG_BLOCK:(n + 1) * SG_BLOCK]).astype(y_ref.dtype)


def _spatial_gate(uv, ln_g, ln_b, w_s_stack, layer, b_s_t, casts):
    return _call(
        _sg_kernel,
        name="spatial_gate",
        grid=(TOKENS // SG_TM,),
        in_specs=[pl.BlockSpec((SG_TM, SG_WIDTH), lambda i: (i, 0)),
                  pl.BlockSpec((SG_TM, SG_WIDTH), lambda i: (i, 1)),
                  pl.BlockSpec((1, SG_WIDTH), lambda i: (0, 0)),
                  pl.BlockSpec((1, SG_WIDTH), lambda i: (0, 0)),
                  pl.BlockSpec((None, SG_GROUPS, SG_BLOCK, SG_BLOCK), lambda i: (layer, 0, 0, 0)),
                  pl.BlockSpec((SG_BLOCK, SG_GROUPS), lambda i: (0, 0))],
        out_specs=[pl.BlockSpec((SG_TM, SG_WIDTH), lambda i: (i, 0))],
        out_shape=[jax.ShapeDtypeStruct((TOKENS, SG_WIDTH), BF16)],
        scratch_shapes=[pltpu.VMEM((SG_TM, SG_WIDTH), BF16)],
        args=(uv, uv, ln_g, ln_b, w_s_stack, b_s_t),
        casts=casts)


def _row(v):
    return v.reshape(1, -1)


def kernel(x, p, positions, ev_norm_pre, ev_w_in, ev_lam_q1, ev_lam_k1, ev_lam_q2, ev_lam_k2,
           ev_subln, ev_w_out, ev_norm_post, od_norm_pre, od_w_in, od_ln_g, od_ln_b, od_w_s,
           od_b_s, od_w_out, od_norm_post, ffn_norm_pre, ffn_w1, ffn_w2, ffn_norm_post,
           ple_w_proj, ple_w_gate, ple_norm):
    assert DEPTH == 2, "the cast schedule below is written for one even and one odd layer"
    h = x.reshape(TOKENS, D_MODEL)
    p_stack = p.reshape(DEPTH, TOKENS, PLE_DIM)
    cos_t, sin_t = _rope_tables(positions)
    sb_q_scale = jnp.concatenate([jnp.full((1, SB_WIDTH), Q_SCALE, F32),
                                  jnp.ones((1, (len(PLAIN_W_TILES) - 1) * PROJ_TN), F32)], axis=1)
    w_in0 = ev_w_in[0].astype(BF16)

    lambda_init = 0.8 - 0.6 * math.exp(-0.3 * 0)
    (proj, xn), (w_out0,) = _plain_proj(h, _row(ev_norm_pre[0]), w_in0, sb_q_scale,
                                        casts=[_Cast(ev_w_out, 0)])
    (qk,), (ffn_w2_0,) = _rope_proj(xn, w_in0, cos_t, sin_t, casts=[_Cast(ffn_w2, 0)])
    proj3 = proj.reshape(BATCH, SEQ, len(PLAIN_W_TILES) * PROJ_TN)
    qk3 = qk.reshape(BATCH, SEQ, 2 * DIFF_QK_WIDTH)
    (sb_o,), (ffn_w1_0,) = _sb_attention(proj3, casts=[_Cast(ffn_w1, 0)])
    lam_rows = jnp.stack([ev_lam_q1[0], ev_lam_k1[0], ev_lam_q2[0], ev_lam_k2[0]])
    (df_o,), (gate_0, pproj_0) = _diff_attention(
        qk3, proj3, lam_rows, ev_subln[0].reshape(DIFF_V_DIM, 1), lambda_init,
        casts=[_Cast(ple_w_gate, 0), _Cast(ple_w_proj, 0)])
    (h,), _ = _out_proj([sb_o.reshape(TOKENS, SB_WIDTH), df_o.reshape(TOKENS, DIFF_V_WIDTH)],
                        w_out0, _row(ev_norm_post[0]), h, "even_out_proj")
    (h,), _ = _ffn(h, _row(ffn_norm_pre[0]), ffn_w1_0, ffn_w2_0, _row(ffn_norm_post[0]), casts=[])
    (h,), (od_in, od_out) = _ple(h, p_stack, 0, gate_0, pproj_0, _row(ple_norm[0]),
                                 casts=[_Cast(od_w_in, 0), _Cast(od_w_out, 0)])

    (uv,), (ffn_w1_1,) = _odd_proj(h, _row(od_norm_pre[0]), od_in, casts=[_Cast(ffn_w1, 1)])
    (y,), (gate_1, pproj_1) = _spatial_gate(
        uv, _row(od_ln_g[0]), _row(od_ln_b[0]), od_w_s, 0, od_b_s[0].T,
        casts=[_Cast(ple_w_gate, 1), _Cast(ple_w_proj, 1)])
    (h,), (ffn_w2_1,) = _out_proj([y], od_out, _row(od_norm_post[0]), h, "odd_out_proj",
                                  casts=[_Cast(ffn_w2, 1)])
    (h,), _ = _ffn(h, _row(ffn_norm_pre[1]), ffn_w1_1, ffn_w2_1, _row(ffn_norm_post[1]), casts=[])
    (h,), _ = _ple(h, p_stack, 1, gate_1, pproj_1, _row(ple_norm[1]))
    return h.reshape(BATCH, SEQ, D_MODEL)
```

```python
import functools
import math
from typing import NamedTuple

import jax
import jax.numpy as jnp
from jax import lax
from jax.experimental import pallas as pl
from jax.experimental.pallas import tpu as pltpu

D_MODEL = 2048
BATCH = 8
SEQ = 2048
DEPTH = 2
TOKENS = BATCH * SEQ

CHUNK = 64
HEAD_DIM = 128
SB_HEADS = 8
DIFF_HEADS = 4
DIFF_MAPS = 2 * DIFF_HEADS
DIFF_V_DIM = 2 * HEAD_DIM
ROT_DIM = HEAD_DIM // 4
ROT_HALF = ROT_DIM // 2
ROPE_PACK = 128 // ROT_DIM
ROPE_THETA = 500000.0
SG_BLOCK = 128
SG_GROUPS = 16
SG_GROUP_DIM = 128
SG_WIDTH = SG_GROUPS * SG_GROUP_DIM
D_FF = 4 * D_MODEL
PLE_DIM = 256
SB_WIDTH = SB_HEADS * HEAD_DIM
DIFF_QK_WIDTH = DIFF_HEADS * 2 * HEAD_DIM
DIFF_V_WIDTH = DIFF_HEADS * DIFF_V_DIM
EVEN_IN_WIDTH = 3 * SB_WIDTH + 2 * DIFF_QK_WIDTH + DIFF_V_WIDTH
EPS = 1e-6
LOG2E = math.log2(math.e)
Q_SCALE = HEAD_DIM ** -0.5 * LOG2E
EXP2_ZERO_BELOW = -151.0

LANES = 128
BF16_SUBLANES = 16
VMEM_LIMIT_BYTES = 56 * 1024 * 1024

F32 = jnp.float32
BF16 = jnp.bfloat16

PROJ_TM = 1024
PROJ_TN = 1024
ATT_TQ = 256
ATT_BK = 256
SB_NH = 8
OUT_TM = 512
FFN_TM = 512
FFN_TF = 1024
PLE_TM = 512
SG_TM = 1024
ROPE_TM = 2048
NORM_ROWS = 16

PLAIN_W_TILES = (0, 1, 2, 5)
ROPE_W_TILE0 = 3


def _chunk_of(idx):
    return jnp.right_shift(idx, CHUNK.bit_length() - 1)


def _rms_scale(x):
    return lax.rsqrt(jnp.mean(x * x, axis=-1, keepdims=True) + EPS)


class _Cast(NamedTuple):
    stack: jax.Array
    layer: int


def _no_rider():
    pass


def _call(body, *, name, grid, in_specs, out_specs, out_shape, args, scratch_shapes=(), casts=()):
    steps = math.prod(grid)
    n_in, n_out, n_cast = len(in_specs), len(out_specs), len(casts)

    def linear_step(*idx):
        step = idx[0]
        for extent, k in zip(grid[1:], idx[1:]):
            step = step * extent + k
        return step

    cast_in, cast_out, cast_shape = [], [], []
    for job in casts:
        _, rows, cols = job.stack.shape
        assert rows % BF16_SUBLANES == 0
        n_blocks = math.gcd(steps, rows // BF16_SUBLANES)
        blk_rows = rows // n_blocks
        repeat = steps // n_blocks
        cast_in.append(pl.BlockSpec(
            (None, blk_rows, cols),
            functools.partial(lambda *idx, layer, repeat: (layer, linear_step(*idx) // repeat, 0),
                              layer=job.layer, repeat=repeat)))
        cast_out.append(pl.BlockSpec(
            (blk_rows, cols),
            functools.partial(lambda *idx, repeat: (linear_step(*idx) // repeat, 0), repeat=repeat)))
        cast_shape.append(jax.ShapeDtypeStruct((rows, cols), BF16))

    def wrapped(*refs):
        ins = refs[:n_in]
        srcs = refs[n_in:n_in + n_cast]
        outs = refs[n_in + n_cast:n_in + n_cast + n_out]
        dsts = refs[n_in + n_cast + n_out:n_in + 2 * n_cast + n_out]
        scratch = refs[n_in + 2 * n_cast + n_out:]

        def rider():
            for src, dst in zip(srcs, dsts):
                dst[...] = src[...].astype(dst.dtype)

        body(*ins, *outs, *scratch, rider=rider if n_cast else _no_rider)

    results = pl.pallas_call(
        wrapped,
        grid=grid,
        in_specs=list(in_specs) + cast_in,
        out_specs=list(out_specs) + cast_out,
        out_shape=list(out_shape) + cast_shape,
        scratch_shapes=list(scratch_shapes),
        compiler_params=pltpu.CompilerParams(dimension_semantics=("arbitrary",) * len(grid),
                                             vmem_limit_bytes=VMEM_LIMIT_BYTES),
        name=name,
    )(*args, *[job.stack for job in casts])
    return results[:n_out], results[n_out:]


def _rope_table_kernel(pos_ref, freq_ref, cos_ref, sin_ref, *, rider):
    rider()
    ang = pos_ref[...].astype(F32) * freq_ref[...]
    lane = lax.broadcasted_iota(jnp.int32, ang.shape, 1)
    c = jnp.cos(ang)
    s = jnp.sin(ang)
    s = jnp.where(jnp.bitwise_and(lane, ROT_DIM - 1) < ROT_HALF, -s, s)
    rotated = lane < ROT_DIM
    n_rows = ang.shape[0]
    for k in range(ROPE_PACK):
        shift = (LANES - k * ROT_DIM) % LANES
        ck = jnp.where(rotated, pltpu.roll(c, shift, 1) if shift else c, 1.0)
        sk = jnp.where(rotated, pltpu.roll(s, shift, 1) if shift else s, 0.0)
        rows = pl.ds(k, n_rows, stride=ROPE_PACK)
        cos_ref[0, rows, :] = ck * Q_SCALE
        sin_ref[0, rows, :] = sk * Q_SCALE
        cos_ref[1, rows, :] = ck
        sin_ref[1, rows, :] = sk


def _rope_tables(positions, casts):
    inv_freq = ROPE_THETA ** (-jnp.arange(0, ROT_DIM, 2, dtype=F32) / ROT_DIM)
    freq_row = jnp.tile(jnp.concatenate([inv_freq, inv_freq]), ROPE_PACK).reshape(1, LANES)
    pos = jnp.repeat(positions.reshape(TOKENS // ROPE_PACK, ROPE_PACK), ROT_DIM, axis=1)
    return _call(
        _rope_table_kernel,
        name="rope_tables",
        grid=(TOKENS // ROPE_TM,),
        in_specs=[pl.BlockSpec((ROPE_TM // ROPE_PACK, LANES), lambda i: (i, 0)),
                  pl.BlockSpec((1, LANES), lambda i: (0, 0))],
        out_specs=[pl.BlockSpec((2, ROPE_TM, LANES), lambda i: (0, i, 0)),
                   pl.BlockSpec((2, ROPE_TM, LANES), lambda i: (0, i, 0))],
        out_shape=[jax.ShapeDtypeStruct((2, TOKENS, LANES), F32)] * 2,
        args=(pos, freq_row),
        casts=casts)


def _normalize_rows(x_ref, g_ref, xn_ref):
    g = g_ref[...]
    for r0 in range(0, x_ref.shape[0], NORM_ROWS):
        x = x_ref[r0:r0 + NORM_ROWS, :]
        xn_ref[r0:r0 + NORM_ROWS, :] = (x * _rms_scale(x) * g).astype(BF16)


def _add_normalized_rows(h_ref, m_ref, g_ref, o_ref):
    g = g_ref[...]
    for r0 in range(0, h_ref.shape[0], NORM_ROWS):
        rows = slice(r0, r0 + NORM_ROWS)
        m = m_ref[rows, :]
        o_ref[rows, :] = h_ref[rows, :] + m * _rms_scale(m) * g


def _plain_proj_kernel(x_ref, g_ref, w_ref, cs_ref, o_ref, xn_ref, *, rider):
    @pl.when(pl.program_id(1) == 0)
    def _():
        _normalize_rows(x_ref, g_ref, xn_ref)

    rider()
    acc = jnp.dot(xn_ref[...], w_ref[...], preferred_element_type=F32)
    o_ref[...] = (acc * cs_ref[...]).astype(o_ref.dtype)


def _plain_proj(h, g, w, col_scale, casts):
    n_tiles = len(PLAIN_W_TILES)
    first_gap = PLAIN_W_TILES.index(5)

    def w_map(i, j):
        return (0, jnp.where(j < first_gap, j, j + (5 - first_gap)))

    return _call(
        _plain_proj_kernel,
        name="even_plain_proj",
        grid=(TOKENS // PROJ_TM, n_tiles),
        in_specs=[pl.BlockSpec((PROJ_TM, D_MODEL), lambda i, j: (i, 0)),
                  pl.BlockSpec((1, D_MODEL), lambda i, j: (0, 0)),
                  pl.BlockSpec((D_MODEL, PROJ_TN), w_map),
                  pl.BlockSpec((1, PROJ_TN), lambda i, j: (0, j))],
        out_specs=[pl.BlockSpec((PROJ_TM, PROJ_TN), lambda i, j: (i, j)),
                   pl.BlockSpec((PROJ_TM, D_MODEL), lambda i, j: (i, 0))],
        out_shape=[jax.ShapeDtypeStruct((TOKENS, n_tiles * PROJ_TN), BF16),
                   jax.ShapeDtypeStruct((TOKENS, D_MODEL), BF16)],
        args=(h, g, w, col_scale),
        casts=casts)


def _rope_proj_kernel(xn_ref, w_ref, cos_ref, sin_ref, o_ref, *, rider):
    rider()
    acc = jnp.dot(xn_ref[...], w_ref[...], preferred_element_type=F32)
    c = cos_ref[...]
    s = sin_ref[...]
    first = lax.broadcasted_iota(jnp.int32, c.shape, 1) < ROT_HALF
    for g in range(PROJ_TN // LANES):
        a = acc[:, g * LANES:(g + 1) * LANES]
        swapped = jnp.where(first, pltpu.roll(a, LANES - ROT_HALF, 1), pltpu.roll(a, ROT_HALF, 1))
        o_ref[:, g * LANES:(g + 1) * LANES] = (a * c + swapped * s).astype(o_ref.dtype)


def _rope_proj(xn, w, cos_t, sin_t, casts):
    return _call(
        _rope_proj_kernel,
        name="even_rope_proj",
        grid=(TOKENS // PROJ_TM, 2),
        in_specs=[pl.BlockSpec((PROJ_TM, D_MODEL), lambda i, j: (i, 0)),
                  pl.BlockSpec((D_MODEL, PROJ_TN), lambda i, j: (0, ROPE_W_TILE0 + j)),
                  pl.BlockSpec((None, PROJ_TM, LANES), lambda i, j: (j, i, 0)),
                  pl.BlockSpec((None, PROJ_TM, LANES), lambda i, j: (j, i, 0))],
        out_specs=[pl.BlockSpec((PROJ_TM, PROJ_TN), lambda i, j: (i, j))],
        out_shape=[jax.ShapeDtypeStruct((TOKENS, 2 * PROJ_TN), BF16)],
        args=(xn, w, cos_t, sin_t),
        casts=casts)


def _gelu_tanh(x):
    c = -2.0 * math.sqrt(2.0 / math.pi) * LOG2E
    return x / (1.0 + jnp.exp2(x * (c + (0.044715 * c) * (x * x))))


def _odd_proj_kernel(x_ref, g_ref, w_ref, o_ref, xn_ref, *, rider):
    @pl.when(pl.program_id(1) == 0)
    def _():
        _normalize_rows(x_ref, g_ref, xn_ref)

    rider()
    acc = jnp.dot(xn_ref[...], w_ref[...], preferred_element_type=F32)
    o_ref[...] = _gelu_tanh(acc).astype(o_ref.dtype)


def _odd_proj(h, g, w, casts):
    n = w.shape[1]
    return _call(
        _odd_proj_kernel,
        name="odd_in_proj",
        grid=(TOKENS // PROJ_TM, n // PROJ_TN),
        in_specs=[pl.BlockSpec((PROJ_TM, D_MODEL), lambda i, j: (i, 0)),
                  pl.BlockSpec((1, D_MODEL), lambda i, j: (0, 0)),
                  pl.BlockSpec((D_MODEL, PROJ_TN), lambda i, j: (0, j))],
        out_specs=[pl.BlockSpec((PROJ_TM, PROJ_TN), lambda i, j: (i, j))],
        out_shape=[jax.ShapeDtypeStruct((TOKENS, n), BF16)],
        scratch_shapes=[pltpu.VMEM((PROJ_TM, D_MODEL), BF16)],
        args=(h, g, w),
        casts=casts)


def _store_transposed_values(v_ref, vt_ref):
    for jb in range(SEQ // ATT_BK):
        vt_ref[jb] = v_ref[jb * ATT_BK:(jb + 1) * ATT_BK, :].astype(F32).T.astype(BF16)


def _scores_t(k_ref, q_ref, j, cols):
    kj = k_ref[pl.ds(j * ATT_BK, ATT_BK), cols]
    return lax.dot_general(kj, q_ref[:, cols], (((1,), (1,)), ((), ())),
                           preferred_element_type=F32)


def _sb_attn_kernel(q_ref, k_ref, v_ref, o_ref, vt_ref, acc_ref, carry_ref, *, rider):
    i = pl.program_id(2)

    @pl.when(i == 0)
    def _():
        _store_transposed_values(v_ref, vt_ref)

    rider()
    key = lax.broadcasted_iota(jnp.int32, (ATT_BK, ATT_TQ), 0)
    qry = lax.broadcasted_iota(jnp.int32, (ATT_BK, ATT_TQ), 1)
    causal = key < qry
    later_key = lax.broadcasted_iota(jnp.int32, (ATT_BK, ATT_BK), 1)
    this_key = lax.broadcasted_iota(jnp.int32, (ATT_BK, ATT_BK), 0)
    tri = (later_key > this_key).astype(BF16)

    heads = range(SB_NH)

    def head_cols(h):
        return slice(h * HEAD_DIM, (h + 1) * HEAD_DIM)

    def blocks(j, carries, diag):
        zs = [_scores_t(k_ref, q_ref, j, head_cols(h)) for h in heads]
        log_betas, log_keeps = [], []
        for z in zs:
            soft = jnp.log2(1.0 + jnp.exp2(-jnp.abs(z)))
            log_beta = jnp.minimum(z, 0.0) - soft
            log_keep = log_beta - z
            if diag:
                log_keep = jnp.where(causal, log_keep, 0.0)
            log_betas.append(log_beta)
            log_keeps.append(log_keep)
        laters = [jnp.dot(tri, lk.astype(BF16), preferred_element_type=F32) for lk in log_keeps]
        ws = []
        for h in heads:
            arg = log_betas[h] + laters[h]
            if carries is not None:
                arg = arg + carries[h]
            w = jnp.exp2(arg)
            if diag:
                w = jnp.where(causal, w, 0.0)
            ws.append(w.astype(BF16))
        contribs = [jnp.dot(vt_ref[j, head_cols(h), :], ws[h], preferred_element_type=F32)
                    for h in heads]
        colsums = [laters[h][0:1, :] + log_keeps[h][0:1, :] for h in heads]
        return contribs, colsums

    def any_weight_left(carries):
        worst = functools.reduce(jnp.maximum, carries)
        return (jnp.max(worst) > EXP2_ZERO_BELOW).astype(jnp.int32)

    contribs, colsums = blocks(i, None, True)
    for h in heads:
        acc_ref[h] = contribs[h]
        carry_ref[h] = colsums[h]

    def cond(state):
        j, more = state
        return jnp.logical_and(j >= 0, more > 0)

    def body(state):
        j, _ = state
        carries = [carry_ref[h] for h in heads]
        contribs, colsums = blocks(j, carries, False)
        carries = [carries[h] + colsums[h] for h in heads]
        for h in heads:
            acc_ref[h] += contribs[h]
            carry_ref[h] = carries[h]
        return j - 1, any_weight_left(carries)

    lax.while_loop(cond, body, (i - 1, any_weight_left(colsums)))
    for h in range(SB_NH):
        o_ref[:, h * HEAD_DIM:(h + 1) * HEAD_DIM] = acc_ref[h].T.astype(o_ref.dtype)


def _sb_attention(proj3, casts):
    width = SB_NH * HEAD_DIM
    groups = SB_HEADS // SB_NH
    return _call(
        _sb_attn_kernel,
        name="sb_attention",
        grid=(BATCH, groups, SEQ // ATT_TQ),
        in_specs=[pl.BlockSpec((None, ATT_TQ, width), lambda b, g, i: (b, i, g)),
                  pl.BlockSpec((None, SEQ, width), lambda b, g, i: (b, 0, groups + g)),
                  pl.BlockSpec((None, SEQ, width), lambda b, g, i: (b, 0, 2 * groups + g))],
        out_specs=[pl.BlockSpec((None, ATT_TQ, width), lambda b, g, i: (b, i, g))],
        out_shape=[jax.ShapeDtypeStruct((BATCH, SEQ, SB_WIDTH), BF16)],
        scratch_shapes=[pltpu.VMEM((SEQ // ATT_BK, width, ATT_BK), BF16),
                        pltpu.VMEM((SB_NH, HEAD_DIM, ATT_TQ), F32),
                        pltpu.VMEM((SB_NH, 1, ATT_TQ), F32)],
        args=(proj3, proj3, proj3),
        casts=casts)


def _diff_attn_kernel(lam_ref, g_ref, q_ref, k_ref, v_ref, o_ref, vt_ref, acc_ref, m_ref, l_ref,
                      z_ref, *, lambda_init, rider):
    i = pl.program_id(1)

    @pl.when(i == 0)
    def _():
        _store_transposed_values(v_ref, vt_ref)

    rider()
    key = lax.broadcasted_iota(jnp.int32, (ATT_BK, ATT_TQ), 0)
    qry = lax.broadcasted_iota(jnp.int32, (ATT_BK, ATT_TQ), 1)
    visible = _chunk_of(key) <= _chunk_of(qry)

    def head_cols(c):
        return slice(c * HEAD_DIM, (c + 1) * HEAD_DIM)

    def value_rows(c):
        return slice((c // 2) * DIFF_V_DIM, (c // 2 + 1) * DIFF_V_DIM)

    maps = range(DIFF_MAPS)

    def scores_all(j):
        return [_scores_t(k_ref, q_ref, j, head_cols(c)) for c in maps]

    def absorb(j):
        ps, alphas = [], []
        for c in maps:
            z = z_ref[c]
            m_old = m_ref[c]
            m_new = jnp.maximum(m_old, jnp.max(z, axis=0, keepdims=True))
            alpha = jnp.exp2(m_old - m_new)
            p = jnp.exp2(z - m_new)
            m_ref[c] = m_new
            l_ref[c] = l_ref[c] * alpha + jnp.sum(p, axis=0, keepdims=True)
            ps.append(p.astype(BF16))
            alphas.append(alpha)
        pvs = [jnp.dot(vt_ref[j, value_rows(c), :], ps[c], preferred_element_type=F32)
               for c in maps]
        for c in maps:
            acc_ref[c] = acc_ref[c] * alphas[c] + pvs[c]

    for c in maps:
        m_ref[c] = jnp.full((1, ATT_TQ), -jnp.inf, F32)
        l_ref[c] = jnp.zeros((1, ATT_TQ), F32)
        acc_ref[c] = jnp.zeros((DIFF_V_DIM, ATT_TQ), F32)
    for c, z in enumerate(scores_all(i)):
        z_ref[c] = jnp.where(visible, z, -jnp.inf)

    def body(jj, unused):
        j = i - jj
        zs_next = scores_all(j - 1)
        absorb(j)
        for c in maps:
            z_ref[c] = zs_next[c]
        return unused

    lax.fori_loop(0, i, body, 0)
    absorb(0)

    lam_rows = lam_ref[...]
    s1 = jnp.sum(lam_rows[0:1, :] * lam_rows[1:2, :], axis=-1, keepdims=True)
    s2 = jnp.sum(lam_rows[2:3, :] * lam_rows[3:4, :], axis=-1, keepdims=True)
    lam = jnp.exp(s1) - jnp.exp(s2) + lambda_init
    gain = g_ref[...]
    for h in range(DIFF_HEADS):
        o = acc_ref[2 * h] / l_ref[2 * h] - lam * (acc_ref[2 * h + 1] / l_ref[2 * h + 1])
        r = lax.rsqrt(jnp.mean(o * o, axis=0, keepdims=True) + EPS)
        o = o * r * gain * (1.0 - lambda_init)
        o_ref[:, h * DIFF_V_DIM:(h + 1) * DIFF_V_DIM] = o.T.astype(o_ref.dtype)


def _diff_attention(qk3, proj3, lam_rows, subln_col, lambda_init, casts):
    v_tile = PLAIN_W_TILES.index(5)
    return _call(
        functools.partial(_diff_attn_kernel, lambda_init=lambda_init),
        name="diff_attention",
        grid=(BATCH, SEQ // ATT_TQ),
        in_specs=[pl.BlockSpec((4, HEAD_DIM), lambda b, i: (0, 0)),
                  pl.BlockSpec((DIFF_V_DIM, 1), lambda b, i: (0, 0)),
                  pl.BlockSpec((None, ATT_TQ, DIFF_QK_WIDTH), lambda b, i: (b, i, 0)),
                  pl.BlockSpec((None, SEQ, DIFF_QK_WIDTH), lambda b, i: (b, 0, 1)),
                  pl.BlockSpec((None, SEQ, DIFF_V_WIDTH), lambda b, i: (b, 0, v_tile))],
        out_specs=[pl.BlockSpec((None, ATT_TQ, DIFF_V_WIDTH), lambda b, i: (b, i, 0))],
        out_shape=[jax.ShapeDtypeStruct((BATCH, SEQ, DIFF_V_WIDTH), BF16)],
        scratch_shapes=[pltpu.VMEM((SEQ // ATT_BK, DIFF_V_WIDTH, ATT_BK), BF16),
                        pltpu.VMEM((DIFF_MAPS, DIFF_V_DIM, ATT_TQ), F32),
                        pltpu.VMEM((DIFF_MAPS, 1, ATT_TQ), F32),
                        pltpu.VMEM((DIFF_MAPS, 1, ATT_TQ), F32),
                        pltpu.VMEM((DIFF_MAPS, ATT_BK, ATT_TQ), F32)],
        args=(lam_rows, subln_col, qk3, qk3, proj3),
        casts=casts)


def _out_proj_kernel(*refs, n_in, rider):
    a_refs = refs[:n_in]
    w_refs = refs[n_in:2 * n_in]
    g_ref, h_ref, o_ref = refs[2 * n_in:]
    rider()
    m = jnp.dot(a_refs[0][...], w_refs[0][...], preferred_element_type=F32)
    for a_ref, w_ref in zip(a_refs[1:], w_refs[1:]):
        m = m + jnp.dot(a_ref[...], w_ref[...], preferred_element_type=F32)
    o_ref[...] = m
    _add_normalized_rows(h_ref, o_ref, g_ref, o_ref)


def _out_proj(acts, w, g, h, name, casts=()):
    n_in = len(acts)
    k = w.shape[0] // n_in
    in_specs = ([pl.BlockSpec((OUT_TM, k), lambda i: (i, 0)) for _ in acts]
                + [pl.BlockSpec((k, D_MODEL), functools.partial(lambda i, r: (r, 0), r=r))
                   for r in range(n_in)]
                + [pl.BlockSpec((1, D_MODEL), lambda i: (0, 0)),
                   pl.BlockSpec((OUT_TM, D_MODEL), lambda i: (i, 0))])
    return _call(
        functools.partial(_out_proj_kernel, n_in=n_in),
        name=name,
        grid=(TOKENS // OUT_TM,),
        in_specs=in_specs,
        out_specs=[pl.BlockSpec((OUT_TM, D_MODEL), lambda i: (i, 0))],
        out_shape=[jax.ShapeDtypeStruct((TOKENS, D_MODEL), F32)],
        args=(*acts, *([w] * n_in), g, h),
        casts=casts)


def _ffn_kernel(h_ref, g1_ref, w1_ref, w2_ref, g2_ref, o_ref, hn_ref, *, rider):
    f = pl.program_id(1)

    @pl.when(f == 0)
    def _():
        _normalize_rows(h_ref, g1_ref, hn_ref)
        o_ref[...] = jnp.zeros_like(o_ref)

    rider()
    a = jnp.maximum(jnp.dot(hn_ref[...], w1_ref[...], preferred_element_type=F32), 0.0)
    o_ref[...] += jnp.dot((a * a).astype(BF16), w2_ref[...], preferred_element_type=F32)

    @pl.when(f == pl.num_programs(1) - 1)
    def _():
        _add_normalized_rows(h_ref, o_ref, g2_ref, o_ref)


def _ffn(h, g1, w1, w2, g2, casts):
    return _call(
        _ffn_kernel,
        name="ffn",
        grid=(TOKENS // FFN_TM, D_FF // FFN_TF),
        in_specs=[pl.BlockSpec((FFN_TM, D_MODEL), lambda i, f: (i, 0)),
                  pl.BlockSpec((1, D_MODEL), lambda i, f: (0, 0)),
                  pl.BlockSpec((D_MODEL, FFN_TF), lambda i, f: (0, f)),
                  pl.BlockSpec((FFN_TF, D_MODEL), lambda i, f: (f, 0)),
                  pl.BlockSpec((1, D_MODEL), lambda i, f: (0, 0))],
        out_specs=[pl.BlockSpec((FFN_TM, D_MODEL), lambda i, f: (i, 0))],
        out_shape=[jax.ShapeDtypeStruct((TOKENS, D_MODEL), F32)],
        scratch_shapes=[pltpu.VMEM((FFN_TM, D_MODEL), BF16)],
        args=(h, g1, w1, w2, g2),
        casts=casts)


def _ple_kernel(h_ref, p_ref, wg_ref, wp_ref, g_ref, o_ref, *, rider):
    rider()
    z = jnp.dot(h_ref[...].astype(BF16), wg_ref[...], preferred_element_type=F32)
    e = jnp.dot(p_ref[...].astype(BF16), wp_ref[...], preferred_element_type=F32)
    o_ref[...] = e / (1.0 + jnp.exp2(z * (-LOG2E)))
    _add_normalized_rows(h_ref, o_ref, g_ref, o_ref)


def _ple(h, p_stack, layer, wg, wp, g, casts=()):
    return _call(
        _ple_kernel,
        name="ple",
        grid=(TOKENS // PLE_TM,),
        in_specs=[pl.BlockSpec((PLE_TM, D_MODEL), lambda i: (i, 0)),
                  pl.BlockSpec((None, PLE_TM, PLE_DIM), lambda i: (layer, i, 0)),
                  pl.BlockSpec((D_MODEL, D_MODEL), lambda i: (0, 0)),
                  pl.BlockSpec((PLE_DIM, D_MODEL), lambda i: (0, 0)),
                  pl.BlockSpec((1, D_MODEL), lambda i: (0, 0))],
        out_specs=[pl.BlockSpec((PLE_TM, D_MODEL), lambda i: (i, 0))],
        out_shape=[jax.ShapeDtypeStruct((TOKENS, D_MODEL), F32)],
        args=(h, p_stack, wg, wp, g),
        casts=casts)


def _sg_kernel(u_ref, v_ref, lng_ref, lnb_ref, ws_ref, bst_ref, y_ref, vn_ref, *, rider):
    rider()
    ln_g = lng_ref[...]
    ln_b = lnb_ref[...]
    for r0 in range(0, SG_TM, NORM_ROWS):
        rows = slice(r0, r0 + NORM_ROWS)
        v = v_ref[rows, :].astype(F32)
        vc = v - jnp.mean(v, axis=-1, keepdims=True)
        inv = lax.rsqrt(jnp.mean(vc * vc, axis=-1, keepdims=True) + EPS)
        vn_ref[rows, :] = (vc * inv * ln_g + ln_b).astype(BF16)

    n_blk = SG_TM // SG_BLOCK
    t_idx = lax.broadcasted_iota(jnp.int32, (SG_BLOCK, SG_BLOCK), 0)
    s_idx = lax.broadcasted_iota(jnp.int32, (SG_BLOCK, SG_BLOCK), 1)
    visible = _chunk_of(s_idx) <= _chunk_of(t_idx)
    bst = bst_ref[...]
    for g in range(SG_GROUPS):
        cols = slice(g * SG_GROUP_DIM, (g + 1) * SG_GROUP_DIM)
        w = jnp.where(visible, ws_ref[g], 0.0).astype(BF16)
        rhs = jnp.concatenate(
            [vn_ref[n * SG_BLOCK:(n + 1) * SG_BLOCK, cols] for n in range(n_blk)], axis=1)
        mixed = jnp.dot(w, rhs, preferred_element_type=F32) + bst[:, g:g + 1]
        for n in range(n_blk):
            rows = slice(n * SG_BLOCK, (n + 1) * SG_BLOCK)
            u = u_ref[rows, cols].astype(F32)
            y_ref[rows, cols] = (u * mixed[:, n * SG_BLOCK:(n + 1) * SG_BLOCK]).astype(y_ref.dtype)


def _spatial_gate(uv, ln_g, ln_b, w_s_stack, layer, b_s_t, casts):
    return _call(
        _sg_kernel,
        name="spatial_gate",
        grid=(TOKENS // SG_TM,),
        in_specs=[pl.BlockSpec((SG_TM, SG_WIDTH), lambda i: (i, 0)),
                  pl.BlockSpec((SG_TM, SG_WIDTH), lambda i: (i, 1)),
                  pl.BlockSpec((1, SG_WIDTH), lambda i: (0, 0)),
                  pl.BlockSpec((1, SG_WIDTH), lambda i: (0, 0)),
                  pl.BlockSpec((None, SG_GROUPS, SG_BLOCK, SG_BLOCK), lambda i: (layer, 0, 0, 0)),
                  pl.BlockSpec((SG_BLOCK, SG_GROUPS), lambda i: (0, 0))],
        out_specs=[pl.BlockSpec((SG_TM, SG_WIDTH), lambda i: (i, 0))],
        out_shape=[jax.ShapeDtypeStruct((TOKENS, SG_WIDTH), BF16)],
        scratch_shapes=[pltpu.VMEM((SG_TM, SG_WIDTH), BF16)],
        args=(uv, uv, ln_g, ln_b, w_s_stack, b_s_t),
        casts=casts)


def _row(v):
    return v.reshape(1, -1)


def kernel(x, p, positions, ev_norm_pre, ev_w_in, ev_lam_q1, ev_lam_k1, ev_lam_q2, ev_lam_k2,
           ev_subln, ev_w_out, ev_norm_post, od_norm_pre, od_w_in, od_ln_g, od_ln_b, od_w_s,
           od_b_s, od_w_out, od_norm_post, ffn_norm_pre, ffn_w1, ffn_w2, ffn_norm_post,
           ple_w_proj, ple_w_gate, ple_norm):
    assert DEPTH == 2, "the cast schedule below is written for one even and one odd layer"
    h = x.reshape(TOKENS, D_MODEL)
    p_stack = p.reshape(DEPTH, TOKENS, PLE_DIM)
    (cos_t, sin_t), (w_in0,) = _rope_tables(positions, casts=[_Cast(ev_w_in, 0)])
    sb_q_scale = jnp.concatenate([jnp.full((1, SB_WIDTH), Q_SCALE, F32),
                                  jnp.ones((1, (len(PLAIN_W_TILES) - 1) * PROJ_TN), F32)], axis=1)

    lambda_init = 0.8 - 0.6 * math.exp(-0.3 * 0)
    (proj, xn), (w_out0,) = _plain_proj(h, _row(ev_norm_pre[0]), w_in0, sb_q_scale,
                                        casts=[_Cast(ev_w_out, 0)])
    (qk,), (ffn_w2_0,) = _rope_proj(xn, w_in0, cos_t, sin_t, casts=[_Cast(ffn_w2, 0)])
    proj3 = proj.reshape(BATCH, SEQ, len(PLAIN_W_TILES) * PROJ_TN)
    qk3 = qk.reshape(BATCH, SEQ, 2 * DIFF_QK_WIDTH)
    (sb_o,), (ffn_w1_0,) = _sb_attention(proj3, casts=[_Cast(ffn_w1, 0)])
    lam_rows = jnp.stack([ev_lam_q1[0], ev_lam_k1[0], ev_lam_q2[0], ev_lam_k2[0]])
    (df_o,), (gate_0, pproj_0) = _diff_attention(
        qk3, proj3, lam_rows, ev_subln[0].reshape(DIFF_V_DIM, 1), lambda_init,
        casts=[_Cast(ple_w_gate, 0), _Cast(ple_w_proj, 0)])
    (h,), _ = _out_proj([sb_o.reshape(TOKENS, SB_WIDTH), df_o.reshape(TOKENS, DIFF_V_WIDTH)],
                        w_out0, _row(ev_norm_post[0]), h, "even_out_proj")
    (h,), _ = _ffn(h, _row(ffn_norm_pre[0]), ffn_w1_0, ffn_w2_0, _row(ffn_norm_post[0]), casts=[])
    (h,), (od_in, od_out) = _ple(h, p_stack, 0, gate_0, pproj_0, _row(ple_norm[0]),
                                 casts=[_Cast(od_w_in, 0), _Cast(od_w_out, 0)])

    (uv,), (ffn_w1_1,) = _odd_proj(h, _row(od_norm_pre[0]), od_in, casts=[_Cast(ffn_w1, 1)])
    (y,), (gate_1, pproj_1) = _spatial_gate(
        uv, _row(od_ln_g[0]), _row(od_ln_b[0]), od_w_s, 0, od_b_s[0].T,
        casts=[_Cast(ple_w_gate, 1), _Cast(ple_w_proj, 1)])
    (h,), (ffn_w2_1,) = _out_proj([y], od_out, _row(od_norm_post[0]), h, "odd_out_proj",
                                  casts=[_Cast(ffn_w2, 1)])
    (h,), _ = _ffn(h, _row(ffn_norm_pre[1]), ffn_w1_1, ffn_w2_1, _row(ffn_norm_post[1]), casts=[])
    (h,), _ = _ple(h, p_stack, 1, gate_1, pproj_1, _row(ple_norm[1]))
    return h.reshape(BATCH, SEQ, D_MODEL)
```

```python
import functools
import math
from typing import NamedTuple

import jax
import jax.numpy as jnp
from jax import lax
from jax.experimental import pallas as pl
from jax.experimental.pallas import tpu as pltpu

D_MODEL = 2048
BATCH = 8
SEQ = 2048
DEPTH = 2
TOKENS = BATCH * SEQ

CHUNK = 64
HEAD_DIM = 128
SB_HEADS = 8
DIFF_HEADS = 4
DIFF_MAPS = 2 * DIFF_HEADS
DIFF_V_DIM = 2 * HEAD_DIM
ROT_DIM = HEAD_DIM // 4
ROT_HALF = ROT_DIM // 2
ROPE_PACK = 128 // ROT_DIM
ROPE_THETA = 500000.0
SG_BLOCK = 128
SG_GROUPS = 16
SG_GROUP_DIM = 128
SG_WIDTH = SG_GROUPS * SG_GROUP_DIM
D_FF = 4 * D_MODEL
PLE_DIM = 256
SB_WIDTH = SB_HEADS * HEAD_DIM
DIFF_QK_WIDTH = DIFF_HEADS * 2 * HEAD_DIM
DIFF_V_WIDTH = DIFF_HEADS * DIFF_V_DIM
EVEN_IN_WIDTH = 3 * SB_WIDTH + 2 * DIFF_QK_WIDTH + DIFF_V_WIDTH
EPS = 1e-6
LOG2E = math.log2(math.e)
Q_SCALE = HEAD_DIM ** -0.5 * LOG2E
EXP2_ZERO_BELOW = -151.0

LANES = 128
BF16_SUBLANES = 16
VMEM_LIMIT_BYTES = 56 * 1024 * 1024

F32 = jnp.float32
BF16 = jnp.bfloat16

PROJ_TM = 1024
PROJ_TN = 1024
ATT_TQ = 256
ATT_BK = 256
SB_NH = 8
OUT_TM = 512
FFN_TM = 512
FFN_TF = 1024
PLE_TM = 512
SG_TM = 1024
ROPE_TM = 2048
NORM_ROWS = 16

PLAIN_W_TILES = (0, 1, 2, 5)
ROPE_W_TILE0 = 3


def _chunk_of(idx):
    return jnp.right_shift(idx, CHUNK.bit_length() - 1)


def _rms_scale(x):
    return lax.rsqrt(jnp.mean(x * x, axis=-1, keepdims=True) + EPS)


class _Cast(NamedTuple):
    stack: jax.Array
    layer: int


def _no_rider():
    pass


def _call(body, *, name, grid, in_specs, out_specs, out_shape, args, scratch_shapes=(), casts=()):
    steps = math.prod(grid)
    n_in, n_out, n_cast = len(in_specs), len(out_specs), len(casts)

    def linear_step(*idx):
        step = idx[0]
        for extent, k in zip(grid[1:], idx[1:]):
            step = step * extent + k
        return step

    cast_in, cast_out, cast_shape = [], [], []
    for job in casts:
        _, rows, cols = job.stack.shape
        assert rows % BF16_SUBLANES == 0
        n_blocks = math.gcd(steps, rows // BF16_SUBLANES)
        blk_rows = rows // n_blocks
        repeat = steps // n_blocks
        cast_in.append(pl.BlockSpec(
            (None, blk_rows, cols),
            functools.partial(lambda *idx, layer, repeat: (layer, linear_step(*idx) // repeat, 0),
                              layer=job.layer, repeat=repeat)))
        cast_out.append(pl.BlockSpec(
            (blk_rows, cols),
            functools.partial(lambda *idx, repeat: (linear_step(*idx) // repeat, 0), repeat=repeat)))
        cast_shape.append(jax.ShapeDtypeStruct((rows, cols), BF16))

    def wrapped(*refs):
        ins = refs[:n_in]
        srcs = refs[n_in:n_in + n_cast]
        outs = refs[n_in + n_cast:n_in + n_cast + n_out]
        dsts = refs[n_in + n_cast + n_out:n_in + 2 * n_cast + n_out]
        scratch = refs[n_in + 2 * n_cast + n_out:]

        def rider():
            for src, dst in zip(srcs, dsts):
                dst[...] = src[...].astype(dst.dtype)

        body(*ins, *outs, *scratch, rider=rider if n_cast else _no_rider)

    results = pl.pallas_call(
        wrapped,
        grid=grid,
        in_specs=list(in_specs) + cast_in,
        out_specs=list(out_specs) + cast_out,
        out_shape=list(out_shape) + cast_shape,
        scratch_shapes=list(scratch_shapes),
        compiler_params=pltpu.CompilerParams(dimension_semantics=("arbitrary",) * len(grid),
                                             vmem_limit_bytes=VMEM_LIMIT_BYTES),
        name=name,
    )(*args, *[job.stack for job in casts])
    return results[:n_out], results[n_out:]


def _rope_table_kernel(pos_ref, freq_ref, cos_ref, sin_ref, *, rider):
    rider()
    ang = pos_ref[...].astype(F32) * freq_ref[...]
    lane = lax.broadcasted_iota(jnp.int32, ang.shape, 1)
    c = jnp.cos(ang)
    s = jnp.sin(ang)
    s = jnp.where(jnp.bitwise_and(lane, ROT_DIM - 1) < ROT_HALF, -s, s)
    rotated = lane < ROT_DIM
    n_rows = ang.shape[0]
    for k in range(ROPE_PACK):
        shift = (LANES - k * ROT_DIM) % LANES
        ck = jnp.where(rotated, pltpu.roll(c, shift, 1) if shift else c, 1.0)
        sk = jnp.where(rotated, pltpu.roll(s, shift, 1) if shift else s, 0.0)
        rows = pl.ds(k, n_rows, stride=ROPE_PACK)
        cos_ref[0, rows, :] = ck * Q_SCALE
        sin_ref[0, rows, :] = sk * Q_SCALE
        cos_ref[1, rows, :] = ck
        sin_ref[1, rows, :] = sk


def _rope_tables(positions, casts):
    inv_freq = ROPE_THETA ** (-jnp.arange(0, ROT_DIM, 2, dtype=F32) / ROT_DIM)
    freq_row = jnp.tile(jnp.concatenate([inv_freq, inv_freq]), ROPE_PACK).reshape(1, LANES)
    pos = jnp.repeat(positions.reshape(TOKENS // ROPE_PACK, ROPE_PACK), ROT_DIM, axis=1)
    return _call(
        _rope_table_kernel,
        name="rope_tables",
        grid=(TOKENS // ROPE_TM,),
        in_specs=[pl.BlockSpec((ROPE_TM // ROPE_PACK, LANES), lambda i: (i, 0)),
                  pl.BlockSpec((1, LANES), lambda i: (0, 0))],
        out_specs=[pl.BlockSpec((2, ROPE_TM, LANES), lambda i: (0, i, 0)),
                   pl.BlockSpec((2, ROPE_TM, LANES), lambda i: (0, i, 0))],
        out_shape=[jax.ShapeDtypeStruct((2, TOKENS, LANES), F32)] * 2,
        args=(pos, freq_row),
        casts=casts)


def _normalize_rows(x_ref, g_ref, xn_ref):
    g = g_ref[...]
    for r0 in range(0, x_ref.shape[0], NORM_ROWS):
        x = x_ref[r0:r0 + NORM_ROWS, :]
        xn_ref[r0:r0 + NORM_ROWS, :] = (x * _rms_scale(x) * g).astype(BF16)


def _add_normalized_rows(h_ref, m_ref, g_ref, o_ref):
    g = g_ref[...]
    for r0 in range(0, h_ref.shape[0], NORM_ROWS):
        rows = slice(r0, r0 + NORM_ROWS)
        m = m_ref[rows, :]
        o_ref[rows, :] = h_ref[rows, :] + m * _rms_scale(m) * g


def _plain_proj_kernel(x_ref, g_ref, w_ref, cs_ref, o_ref, xn_ref, *, rider):
    @pl.when(pl.program_id(1) == 0)
    def _():
        _normalize_rows(x_ref, g_ref, xn_ref)

    rider()
    acc = jnp.dot(xn_ref[...], w_ref[...], preferred_element_type=F32)
    o_ref[...] = (acc * cs_ref[...]).astype(o_ref.dtype)


def _plain_proj(h, g, w, col_scale, casts):
    n_tiles = len(PLAIN_W_TILES)
    first_gap = PLAIN_W_TILES.index(5)

    def w_map(i, j):
        return (0, jnp.where(j < first_gap, j, j + (5 - first_gap)))

    return _call(
        _plain_proj_kernel,
        name="even_plain_proj",
        grid=(TOKENS // PROJ_TM, n_tiles),
        in_specs=[pl.BlockSpec((PROJ_TM, D_MODEL), lambda i, j: (i, 0)),
                  pl.BlockSpec((1, D_MODEL), lambda i, j: (0, 0)),
                  pl.BlockSpec((D_MODEL, PROJ_TN), w_map),
                  pl.BlockSpec((1, PROJ_TN), lambda i, j: (0, j))],
        out_specs=[pl.BlockSpec((PROJ_TM, PROJ_TN), lambda i, j: (i, j)),
                   pl.BlockSpec((PROJ_TM, D_MODEL), lambda i, j: (i, 0))],
        out_shape=[jax.ShapeDtypeStruct((TOKENS, n_tiles * PROJ_TN), BF16),
                   jax.ShapeDtypeStruct((TOKENS, D_MODEL), BF16)],
        args=(h, g, w, col_scale),
        casts=casts)


def _rope_proj_kernel(xn_ref, w_ref, cos_ref, sin_ref, o_ref, *, rider):
    rider()
    acc = jnp.dot(xn_ref[...], w_ref[...], preferred_element_type=F32)
    c = cos_ref[...]
    s = sin_ref[...]
    first = lax.broadcasted_iota(jnp.int32, c.shape, 1) < ROT_HALF
    for g in range(PROJ_TN // LANES):
        a = acc[:, g * LANES:(g + 1) * LANES]
        swapped = jnp.where(first, pltpu.roll(a, LANES - ROT_HALF, 1), pltpu.roll(a, ROT_HALF, 1))
        o_ref[:, g * LANES:(g + 1) * LANES] = (a * c + swapped * s).astype(o_ref.dtype)


def _rope_proj(xn, w, cos_t, sin_t, casts):
    return _call(
        _rope_proj_kernel,
        name="even_rope_proj",
        grid=(TOKENS // PROJ_TM, 2),
        in_specs=[pl.BlockSpec((PROJ_TM, D_MODEL), lambda i, j: (i, 0)),
                  pl.BlockSpec((D_MODEL, PROJ_TN), lambda i, j: (0, ROPE_W_TILE0 + j)),
                  pl.BlockSpec((None, PROJ_TM, LANES), lambda i, j: (j, i, 0)),
                  pl.BlockSpec((None, PROJ_TM, LANES), lambda i, j: (j, i, 0))],
        out_specs=[pl.BlockSpec((PROJ_TM, PROJ_TN), lambda i, j: (i, j))],
        out_shape=[jax.ShapeDtypeStruct((TOKENS, 2 * PROJ_TN), BF16)],
        args=(xn, w, cos_t, sin_t),
        casts=casts)


def _gelu_tanh(x):
    c = -2.0 * math.sqrt(2.0 / math.pi) * LOG2E
    return x / (1.0 + jnp.exp2(x * (c + (0.044715 * c) * (x * x))))


def _odd_proj_kernel(x_ref, g_ref, w_ref, o_ref, xn_ref, *, rider):
    @pl.when(pl.program_id(1) == 0)
    def _():
        _normalize_rows(x_ref, g_ref, xn_ref)

    rider()
    acc = jnp.dot(xn_ref[...], w_ref[...], preferred_element_type=F32)
    o_ref[...] = _gelu_tanh(acc).astype(o_ref.dtype)


def _odd_proj(h, g, w, casts):
    n = w.shape[1]
    return _call(
        _odd_proj_kernel,
        name="odd_in_proj",
        grid=(TOKENS // PROJ_TM, n // PROJ_TN),
        in_specs=[pl.BlockSpec((PROJ_TM, D_MODEL), lambda i, j: (i, 0)),
                  pl.BlockSpec((1, D_MODEL), lambda i, j: (0, 0)),
                  pl.BlockSpec((D_MODEL, PROJ_TN), lambda i, j: (0, j))],
        out_specs=[pl.BlockSpec((PROJ_TM, PROJ_TN), lambda i, j: (i, j))],
        out_shape=[jax.ShapeDtypeStruct((TOKENS, n), BF16)],
        scratch_shapes=[pltpu.VMEM((PROJ_TM, D_MODEL), BF16)],
        args=(h, g, w),
        casts=casts)


def _store_transposed_values(v_ref, vt_ref):
    for jb in range(SEQ // ATT_BK):
        vt_ref[jb] = v_ref[jb * ATT_BK:(jb + 1) * ATT_BK, :].astype(F32).T.astype(BF16)


def _store_transposed_queries(q_ref, qt_ref):
    qt_ref[...] = q_ref[...].astype(F32).T.astype(BF16)


def _scores_t(k_ref, qt_ref, j, cols):
    kj = k_ref[pl.ds(j * ATT_BK, ATT_BK), cols]
    return jnp.dot(kj, qt_ref[cols, :], preferred_element_type=F32)


def _sb_attn_kernel(q_ref, k_ref, v_ref, o_ref, vt_ref, acc_ref, carry_ref, qt_ref, *, rider):
    i = pl.program_id(2)

    @pl.when(i == 0)
    def _():
        _store_transposed_values(v_ref, vt_ref)

    rider()
    _store_transposed_queries(q_ref, qt_ref)
    key = lax.broadcasted_iota(jnp.int32, (ATT_BK, ATT_TQ), 0)
    qry = lax.broadcasted_iota(jnp.int32, (ATT_BK, ATT_TQ), 1)
    causal = key < qry
    later_key = lax.broadcasted_iota(jnp.int32, (ATT_BK, ATT_BK), 1)
    this_key = lax.broadcasted_iota(jnp.int32, (ATT_BK, ATT_BK), 0)
    tri = (later_key > this_key).astype(BF16)

    heads = range(SB_NH)

    def head_cols(h):
        return slice(h * HEAD_DIM, (h + 1) * HEAD_DIM)

    def blocks(j, carries, diag):
        zs = [_scores_t(k_ref, qt_ref, j, head_cols(h)) for h in heads]
        log_betas, log_keeps = [], []
        for z in zs:
            soft = jnp.log2(1.0 + jnp.exp2(-jnp.abs(z)))
            log_beta = jnp.minimum(z, 0.0) - soft
            log_keep = log_beta - z
            if diag:
                log_keep = jnp.where(causal, log_keep, 0.0)
            log_betas.append(log_beta)
            log_keeps.append(log_keep)
        laters = [jnp.dot(tri, lk.astype(BF16), preferred_element_type=F32) for lk in log_keeps]
        ws = []
        for h in heads:
            arg = log_betas[h] + laters[h]
            if carries is not None:
                arg = arg + carries[h]
            w = jnp.exp2(arg)
            if diag:
                w = jnp.where(causal, w, 0.0)
            ws.append(w.astype(BF16))
        contribs = [jnp.dot(vt_ref[j, head_cols(h), :], ws[h], preferred_element_type=F32)
                    for h in heads]
        colsums = [laters[h][0:1, :] + log_keeps[h][0:1, :] for h in heads]
        return contribs, colsums

    def any_weight_left(carries):
        worst = functools.reduce(jnp.maximum, carries)
        return (jnp.max(worst) > EXP2_ZERO_BELOW).astype(jnp.int32)

    contribs, colsums = blocks(i, None, True)
    for h in heads:
        acc_ref[h] = contribs[h]
        carry_ref[h] = colsums[h]

    def cond(state):
        j, more = state
        return jnp.logical_and(j >= 0, more > 0)

    def body(state):
        j, _ = state
        carries = [carry_ref[h] for h in heads]
        contribs, colsums = blocks(j, carries, False)
        carries = [carries[h] + colsums[h] for h in heads]
        for h in heads:
            acc_ref[h] += contribs[h]
            carry_ref[h] = carries[h]
        return j - 1, any_weight_left(carries)

    lax.while_loop(cond, body, (i - 1, any_weight_left(colsums)))
    for h in range(SB_NH):
        o_ref[:, h * HEAD_DIM:(h + 1) * HEAD_DIM] = acc_ref[h].T.astype(o_ref.dtype)


def _sb_attention(proj3, casts):
    width = SB_NH * HEAD_DIM
    groups = SB_HEADS // SB_NH
    return _call(
        _sb_attn_kernel,
        name="sb_attention",
        grid=(BATCH, groups, SEQ // ATT_TQ),
        in_specs=[pl.BlockSpec((None, ATT_TQ, width), lambda b, g, i: (b, i, g)),
                  pl.BlockSpec((None, SEQ, width), lambda b, g, i: (b, 0, groups + g)),
                  pl.BlockSpec((None, SEQ, width), lambda b, g, i: (b, 0, 2 * groups + g))],
        out_specs=[pl.BlockSpec((None, ATT_TQ, width), lambda b, g, i: (b, i, g))],
        out_shape=[jax.ShapeDtypeStruct((BATCH, SEQ, SB_WIDTH), BF16)],
        scratch_shapes=[pltpu.VMEM((SEQ // ATT_BK, width, ATT_BK), BF16),
                        pltpu.VMEM((SB_NH, HEAD_DIM, ATT_TQ), F32),
                        pltpu.VMEM((SB_NH, 1, ATT_TQ), F32),
                        pltpu.VMEM((width, ATT_TQ), BF16)],
        args=(proj3, proj3, proj3),
        casts=casts)


def _diff_attn_kernel(lam_ref, g_ref, q_ref, k_ref, v_ref, o_ref, vt_ref, acc_ref, m_ref, l_ref,
                      z_ref, qt_ref, *, lambda_init, rider):
    i = pl.program_id(1)

    @pl.when(i == 0)
    def _():
        _store_transposed_values(v_ref, vt_ref)

    rider()
    _store_transposed_queries(q_ref, qt_ref)
    key = lax.broadcasted_iota(jnp.int32, (ATT_BK, ATT_TQ), 0)
    qry = lax.broadcasted_iota(jnp.int32, (ATT_BK, ATT_TQ), 1)
    visible = _chunk_of(key) <= _chunk_of(qry)

    def head_cols(c):
        return slice(c * HEAD_DIM, (c + 1) * HEAD_DIM)

    def value_rows(c):
        return slice((c // 2) * DIFF_V_DIM, (c // 2 + 1) * DIFF_V_DIM)

    maps = range(DIFF_MAPS)

    def scores_all(j):
        return [_scores_t(k_ref, qt_ref, j, head_cols(c)) for c in maps]

    def absorb(j):
        ps, alphas = [], []
        for c in maps:
            z = z_ref[c]
            m_old = m_ref[c]
            m_new = jnp.maximum(m_old, jnp.max(z, axis=0, keepdims=True))
            alpha = jnp.exp2(m_old - m_new)
            p = jnp.exp2(z - m_new)
            m_ref[c] = m_new
            l_ref[c] = l_ref[c] * alpha + jnp.sum(p, axis=0, keepdims=True)
            ps.append(p.astype(BF16))
            alphas.append(alpha)
        pvs = [jnp.dot(vt_ref[j, value_rows(c), :], ps[c], preferred_element_type=F32)
               for c in maps]
        for c in maps:
            acc_ref[c] = acc_ref[c] * alphas[c] + pvs[c]

    for c in maps:
        m_ref[c] = jnp.full((1, ATT_TQ), -jnp.inf, F32)
        l_ref[c] = jnp.zeros((1, ATT_TQ), F32)
        acc_ref[c] = jnp.zeros((DIFF_V_DIM, ATT_TQ), F32)
    for c, z in enumerate(scores_all(i)):
        z_ref[c] = jnp.where(visible, z, -jnp.inf)

    def body(jj, unused):
        j = i - jj
        zs_next = scores_all(j - 1)
        absorb(j)
        for c in maps:
            z_ref[c] = zs_next[c]
        return unused

    lax.fori_loop(0, i, body, 0)
    absorb(0)

    lam_rows = lam_ref[...]
    s1 = jnp.sum(lam_rows[0:1, :] * lam_rows[1:2, :], axis=-1, keepdims=True)
    s2 = jnp.sum(lam_rows[2:3, :] * lam_rows[3:4, :], axis=-1, keepdims=True)
    lam = jnp.exp(s1) - jnp.exp(s2) + lambda_init
    gain = g_ref[...]
    for h in range(DIFF_HEADS):
        o = acc_ref[2 * h] / l_ref[2 * h] - lam * (acc_ref[2 * h + 1] / l_ref[2 * h + 1])
        r = lax.rsqrt(jnp.mean(o * o, axis=0, keepdims=True) + EPS)
        o = o * r * gain * (1.0 - lambda_init)
        o_ref[:, h * DIFF_V_DIM:(h + 1) * DIFF_V_DIM] = o.T.astype(o_ref.dtype)


def _diff_attention(qk3, proj3, lam_rows, subln_col, lambda_init, casts):
    v_tile = PLAIN_W_TILES.index(5)
    return _call(
        functools.partial(_diff_attn_kernel, lambda_init=lambda_init),
        name="diff_attention",
        grid=(BATCH, SEQ // ATT_TQ),
        in_specs=[pl.BlockSpec((4, HEAD_DIM), lambda b, i: (0, 0)),
                  pl.BlockSpec((DIFF_V_DIM, 1), lambda b, i: (0, 0)),
                  pl.BlockSpec((None, ATT_TQ, DIFF_QK_WIDTH), lambda b, i: (b, i, 0)),
                  pl.BlockSpec((None, SEQ, DIFF_QK_WIDTH), lambda b, i: (b, 0, 1)),
                  pl.BlockSpec((None, SEQ, DIFF_V_WIDTH), lambda b, i: (b, 0, v_tile))],
        out_specs=[pl.BlockSpec((None, ATT_TQ, DIFF_V_WIDTH), lambda b, i: (b, i, 0))],
        out_shape=[jax.ShapeDtypeStruct((BATCH, SEQ, DIFF_V_WIDTH), BF16)],
        scratch_shapes=[pltpu.VMEM((SEQ // ATT_BK, DIFF_V_WIDTH, ATT_BK), BF16),
                        pltpu.VMEM((DIFF_MAPS, DIFF_V_DIM, ATT_TQ), F32),
                        pltpu.VMEM((DIFF_MAPS, 1, ATT_TQ), F32),
                        pltpu.VMEM((DIFF_MAPS, 1, ATT_TQ), F32),
                        pltpu.VMEM((DIFF_MAPS, ATT_BK, ATT_TQ), F32),
                        pltpu.VMEM((DIFF_QK_WIDTH, ATT_TQ), BF16)],
        args=(lam_rows, subln_col, qk3, qk3, proj3),
        casts=casts)


def _out_proj_kernel(*refs, n_in, rider):
    a_refs = refs[:n_in]
    w_refs = refs[n_in:2 * n_in]
    g_ref, h_ref, o_ref = refs[2 * n_in:]
    rider()
    m = jnp.dot(a_refs[0][...], w_refs[0][...], preferred_element_type=F32)
    for a_ref, w_ref in zip(a_refs[1:], w_refs[1:]):
        m = m + jnp.dot(a_ref[...], w_ref[...], preferred_element_type=F32)
    o_ref[...] = m
    _add_normalized_rows(h_ref, o_ref, g_ref, o_ref)


def _out_proj(acts, w, g, h, name, casts=()):
    n_in = len(acts)
    k = w.shape[0] // n_in
    in_specs = ([pl.BlockSpec((OUT_TM, k), lambda i: (i, 0)) for _ in acts]
                + [pl.BlockSpec((k, D_MODEL), functools.partial(lambda i, r: (r, 0), r=r))
                   for r in range(n_in)]
                + [pl.BlockSpec((1, D_MODEL), lambda i: (0, 0)),
                   pl.BlockSpec((OUT_TM, D_MODEL), lambda i: (i, 0))])
    return _call(
        functools.partial(_out_proj_kernel, n_in=n_in),
        name=name,
        grid=(TOKENS // OUT_TM,),
        in_specs=in_specs,
        out_specs=[pl.BlockSpec((OUT_TM, D_MODEL), lambda i: (i, 0))],
        out_shape=[jax.ShapeDtypeStruct((TOKENS, D_MODEL), F32)],
        args=(*acts, *([w] * n_in), g, h),
        casts=casts)


def _ffn_kernel(h_ref, g1_ref, w1_ref, w2_ref, g2_ref, o_ref, hn_ref, *, rider):
    f = pl.program_id(1)

    @pl.when(f == 0)
    def _():
        _normalize_rows(h_ref, g1_ref, hn_ref)
        o_ref[...] = jnp.zeros_like(o_ref)

    rider()
    a = jnp.maximum(jnp.dot(hn_ref[...], w1_ref[...], preferred_element_type=F32), 0.0)
    o_ref[...] += jnp.dot((a * a).astype(BF16), w2_ref[...], preferred_element_type=F32)

    @pl.when(f == pl.num_programs(1) - 1)
    def _():
        _add_normalized_rows(h_ref, o_ref, g2_ref, o_ref)


def _ffn(h, g1, w1, w2, g2, casts):
    return _call(
        _ffn_kernel,
        name="ffn",
        grid=(TOKENS // FFN_TM, D_FF // FFN_TF),
        in_specs=[pl.BlockSpec((FFN_TM, D_MODEL), lambda i, f: (i, 0)),
                  pl.BlockSpec((1, D_MODEL), lambda i, f: (0, 0)),
                  pl.BlockSpec((D_MODEL, FFN_TF), lambda i, f: (0, f)),
                  pl.BlockSpec((FFN_TF, D_MODEL), lambda i, f: (f, 0)),
                  pl.BlockSpec((1, D_MODEL), lambda i, f: (0, 0))],
        out_specs=[pl.BlockSpec((FFN_TM, D_MODEL), lambda i, f: (i, 0))],
        out_shape=[jax.ShapeDtypeStruct((TOKENS, D_MODEL), F32)],
        scratch_shapes=[pltpu.VMEM((FFN_TM, D_MODEL), BF16)],
        args=(h, g1, w1, w2, g2),
        casts=casts)


def _ple_kernel(h_ref, p_ref, wg_ref, wp_ref, g_ref, o_ref, *, rider):
    rider()
    z = jnp.dot(h_ref[...].astype(BF16), wg_ref[...], preferred_element_type=F32)
    e = jnp.dot(p_ref[...].astype(BF16), wp_ref[...], preferred_element_type=F32)
    o_ref[...] = e / (1.0 + jnp.exp2(z * (-LOG2E)))
    _add_normalized_rows(h_ref, o_ref, g_ref, o_ref)


def _ple(h, p_stack, layer, wg, wp, g, casts=()):
    return _call(
        _ple_kernel,
        name="ple",
        grid=(TOKENS // PLE_TM,),
        in_specs=[pl.BlockSpec((PLE_TM, D_MODEL), lambda i: (i, 0)),
                  pl.BlockSpec((None, PLE_TM, PLE_DIM), lambda i: (layer, i, 0)),
                  pl.BlockSpec((D_MODEL, D_MODEL), lambda i: (0, 0)),
                  pl.BlockSpec((PLE_DIM, D_MODEL), lambda i: (0, 0)),
                  pl.BlockSpec((1, D_MODEL), lambda i: (0, 0))],
        out_specs=[pl.BlockSpec((PLE_TM, D_MODEL), lambda i: (i, 0))],
        out_shape=[jax.ShapeDtypeStruct((TOKENS, D_MODEL), F32)],
        args=(h, p_stack, wg, wp, g),
        casts=casts)


def _sg_kernel(u_ref, v_ref, lng_ref, lnb_ref, ws_ref, bst_ref, y_ref, vn_ref, *, rider):
    rider()
    ln_g = lng_ref[...]
    ln_b = lnb_ref[...]
    for r0 in range(0, SG_TM, NORM_ROWS):
        rows = slice(r0, r0 + NORM_ROWS)
        v = v_ref[rows, :].astype(F32)
        vc = v - jnp.mean(v, axis=-1, keepdims=True)
        inv = lax.rsqrt(jnp.mean(vc * vc, axis=-1, keepdims=True) + EPS)
        vn_ref[rows, :] = (vc * inv * ln_g + ln_b).astype(BF16)

    n_blk = SG_TM // SG_BLOCK
    t_idx = lax.broadcasted_iota(jnp.int32, (SG_BLOCK, SG_BLOCK), 0)
    s_idx = lax.broadcasted_iota(jnp.int32, (SG_BLOCK, SG_BLOCK), 1)
    visible = _chunk_of(s_idx) <= _chunk_of(t_idx)
    bst = bst_ref[...]
    for g in range(SG_GROUPS):
        cols = slice(g * SG_GROUP_DIM, (g + 1) * SG_GROUP_DIM)
        w = jnp.where(visible, ws_ref[g], 0.0).astype(BF16)
        rhs = jnp.concatenate(
            [vn_ref[n * SG_BLOCK:(n + 1) * SG_BLOCK, cols] for n in range(n_blk)], axis=1)
        mixed = jnp.dot(w, rhs, preferred_element_type=F32) + bst[:, g:g + 1]
        for n in range(n_blk):
            rows = slice(n * SG_BLOCK, (n + 1) * SG_BLOCK)
            u = u_ref[rows, cols].astype(F32)
            y_ref[rows, cols] = (u * mixed[:, n * SG_BLOCK:(n + 1) * SG_BLOCK]).astype(y_ref.dtype)


def _spatial_gate(uv, ln_g, ln_b, w_s_stack, layer, b_s_t, casts):
    return _call(
        _sg_kernel,
        name="spatial_gate",
        grid=(TOKENS // SG_TM,),
        in_specs=[pl.BlockSpec((SG_TM, SG_WIDTH), lambda i: (i, 0)),
                  pl.BlockSpec((SG_TM, SG_WIDTH), lambda i: (i, 1)),
                  pl.BlockSpec((1, SG_WIDTH), lambda i: (0, 0)),
                  pl.BlockSpec((1, SG_WIDTH), lambda i: (0, 0)),
                  pl.BlockSpec((None, SG_GROUPS, SG_BLOCK, SG_BLOCK), lambda i: (layer, 0, 0, 0)),
                  pl.BlockSpec((SG_BLOCK, SG_GROUPS), lambda i: (0, 0))],
        out_specs=[pl.BlockSpec((SG_TM, SG_WIDTH), lambda i: (i, 0))],
        out_shape=[jax.ShapeDtypeStruct((TOKENS, SG_WIDTH), BF16)],
        scratch_shapes=[pltpu.VMEM((SG_TM, SG_WIDTH), BF16)],
        args=(uv, uv, ln_g, ln_b, w_s_stack, b_s_t),
        casts=casts)


def _row(v):
    return v.reshape(1, -1)


def kernel(x, p, positions, ev_norm_pre, ev_w_in, ev_lam_q1, ev_lam_k1, ev_lam_q2, ev_lam_k2,
           ev_subln, ev_w_out, ev_norm_post, od_norm_pre, od_w_in, od_ln_g, od_ln_b, od_w_s,
           od_b_s, od_w_out, od_norm_post, ffn_norm_pre, ffn_w1, ffn_w2, ffn_norm_post,
           ple_w_proj, ple_w_gate, ple_norm):
    assert DEPTH == 2, "the cast schedule below is written for one even and one odd layer"
    h = x.reshape(TOKENS, D_MODEL)
    p_stack = p.reshape(DEPTH, TOKENS, PLE_DIM)
    (cos_t, sin_t), (w_in0,) = _rope_tables(positions, casts=[_Cast(ev_w_in, 0)])
    sb_q_scale = jnp.concatenate([jnp.full((1, SB_WIDTH), Q_SCALE, F32),
                                  jnp.ones((1, (len(PLAIN_W_TILES) - 1) * PROJ_TN), F32)], axis=1)

    lambda_init = 0.8 - 0.6 * math.exp(-0.3 * 0)
    (proj, xn), (w_out0,) = _plain_proj(h, _row(ev_norm_pre[0]), w_in0, sb_q_scale,
                                        casts=[_Cast(ev_w_out, 0)])
    (qk,), (ffn_w2_0,) = _rope_proj(xn, w_in0, cos_t, sin_t, casts=[_Cast(ffn_w2, 0)])
    proj3 = proj.reshape(BATCH, SEQ, len(PLAIN_W_TILES) * PROJ_TN)
    qk3 = qk.reshape(BATCH, SEQ, 2 * DIFF_QK_WIDTH)
    (sb_o,), (ffn_w1_0,) = _sb_attention(proj3, casts=[_Cast(ffn_w1, 0)])
    lam_rows = jnp.stack([ev_lam_q1[0], ev_lam_k1[0], ev_lam_q2[0], ev_lam_k2[0]])
    (df_o,), (gate_0, pproj_0) = _diff_attention(
        qk3, proj3, lam_rows, ev_subln[0].reshape(DIFF_V_DIM, 1), lambda_init,
        casts=[_Cast(ple_w_gate, 0), _Cast(ple_w_proj, 0)])
    (h,), _ = _out_proj([sb_o.reshape(TOKENS, SB_WIDTH), df_o.reshape(TOKENS, DIFF_V_WIDTH)],
                        w_out0, _row(ev_norm_post[0]), h, "even_out_proj")
    (h,), _ = _ffn(h, _row(ffn_norm_pre[0]), ffn_w1_0, ffn_w2_0, _row(ffn_norm_post[0]), casts=[])
    (h,), (od_in, od_out) = _ple(h, p_stack, 0, gate_0, pproj_0, _row(ple_norm[0]),
                                 casts=[_Cast(od_w_in, 0), _Cast(od_w_out, 0)])

    (uv,), (ffn_w1_1,) = _odd_proj(h, _row(od_norm_pre[0]), od_in, casts=[_Cast(ffn_w1, 1)])
    (y,), (gate_1, pproj_1) = _spatial_gate(
        uv, _row(od_ln_g[0]), _row(od_ln_b[0]), od_w_s, 0, od_b_s[0].T,
        casts=[_Cast(ple_w_gate, 1), _Cast(ple_w_proj, 1)])
    (h,), (ffn_w2_1,) = _out_proj([y], od_out, _row(od_norm_post[0]), h, "odd_out_proj",
                                  casts=[_Cast(ffn_w2, 1)])
    (h,), _ = _ffn(h, _row(ffn_norm_pre[1]), ffn_w1_1, ffn_w2_1, _row(ffn_norm_post[1]), casts=[])
    (h,), _ = _ple(h, p_stack, 1, gate_1, pproj_1, _row(ple_norm[1]))
    return h.reshape(BATCH, SEQ, D_MODEL)
```

```python
import functools
import math
from typing import NamedTuple

import jax
import jax.numpy as jnp
from jax import lax
from jax.experimental import pallas as pl
from jax.experimental.pallas import tpu as pltpu

D_MODEL = 2048
BATCH = 8
SEQ = 2048
DEPTH = 2
TOKENS = BATCH * SEQ

CHUNK = 64
HEAD_DIM = 128
SB_HEADS = 8
DIFF_HEADS = 4
DIFF_MAPS = 2 * DIFF_HEADS
DIFF_V_DIM = 2 * HEAD_DIM
ROT_DIM = HEAD_DIM // 4
ROT_HALF = ROT_DIM // 2
LANES = 128
ROPE_PACK = LANES // ROT_DIM
ROPE_THETA = 500000.0
SG_BLOCK = 128
SG_GROUPS = 16
SG_GROUP_DIM = 128
SG_WIDTH = SG_GROUPS * SG_GROUP_DIM
D_FF = 4 * D_MODEL
PLE_DIM = 256
SB_WIDTH = SB_HEADS * HEAD_DIM
DIFF_QK_WIDTH = DIFF_HEADS * 2 * HEAD_DIM
DIFF_V_WIDTH = DIFF_HEADS * DIFF_V_DIM
EVEN_IN_WIDTH = 3 * SB_WIDTH + 2 * DIFF_QK_WIDTH + DIFF_V_WIDTH
EPS = 1e-6
LOG2E = math.log2(math.e)
Q_SCALE = HEAD_DIM ** -0.5 * LOG2E
EXP2_ZERO_BELOW = -151.0

BF16_SUBLANES = 16
VMEM_LIMIT_BYTES = 56 * 1024 * 1024

F32 = jnp.float32
BF16 = jnp.bfloat16

PROJ_TM = 1024
PROJ_TN = 1024
ATT_TQ = 256
ATT_BK = 256
SB_NH = 8
OUT_TM = 512
FFN_TM = 512
FFN_TF = 1024
PLE_TM = 512
SG_TM = 1024
ROPE_TM = 2048
NORM_ROWS = 16

PLAIN_W_TILES = (0, 1, 2, 5)
ROPE_W_TILE0 = 3


def _chunk_of(idx):
    return jnp.right_shift(idx, CHUNK.bit_length() - 1)


def _rms_scale(x):
    return lax.rsqrt(jnp.mean(x * x, axis=-1, keepdims=True) + EPS)


class _Cast(NamedTuple):
    stack: jax.Array
    layer: int


def _no_rider():
    pass


def _call(body, *, name, grid, in_specs, out_specs, out_shape, args, scratch_shapes=(), casts=()):
    steps = math.prod(grid)
    n_in, n_out, n_cast = len(in_specs), len(out_specs), len(casts)

    def linear_step(*idx):
        step = idx[0]
        for extent, k in zip(grid[1:], idx[1:]):
            step = step * extent + k
        return step

    cast_in, cast_out, cast_shape = [], [], []
    for job in casts:
        _, rows, cols = job.stack.shape
        assert rows % BF16_SUBLANES == 0
        n_blocks = math.gcd(steps, rows // BF16_SUBLANES)
        blk_rows = rows // n_blocks
        repeat = steps // n_blocks
        cast_in.append(pl.BlockSpec(
            (None, blk_rows, cols),
            functools.partial(lambda *idx, layer, repeat: (layer, linear_step(*idx) // repeat, 0),
                              layer=job.layer, repeat=repeat)))
        cast_out.append(pl.BlockSpec(
            (blk_rows, cols),
            functools.partial(lambda *idx, repeat: (linear_step(*idx) // repeat, 0), repeat=repeat)))
        cast_shape.append(jax.ShapeDtypeStruct((rows, cols), BF16))

    def wrapped(*refs):
        ins = refs[:n_in]
        srcs = refs[n_in:n_in + n_cast]
        outs = refs[n_in + n_cast:n_in + n_cast + n_out]
        dsts = refs[n_in + n_cast + n_out:n_in + 2 * n_cast + n_out]
        scratch = refs[n_in + 2 * n_cast + n_out:]

        def rider():
            for src, dst in zip(srcs, dsts):
                dst[...] = src[...].astype(dst.dtype)

        body(*ins, *outs, *scratch, rider=rider if n_cast else _no_rider)

    results = pl.pallas_call(
        wrapped,
        grid=grid,
        in_specs=list(in_specs) + cast_in,
        out_specs=list(out_specs) + cast_out,
        out_shape=list(out_shape) + cast_shape,
        scratch_shapes=list(scratch_shapes),
        compiler_params=pltpu.CompilerParams(dimension_semantics=("arbitrary",) * len(grid),
                                             vmem_limit_bytes=VMEM_LIMIT_BYTES),
        name=name,
    )(*args, *[job.stack for job in casts])
    return results[:n_out], results[n_out:]


def _rope_table_kernel(pos_ref, freq_ref, cos_ref, sin_ref, *, rider):
    rider()
    ang = pos_ref[...].astype(F32) * freq_ref[...]
    lane = lax.broadcasted_iota(jnp.int32, ang.shape, 1)
    c = jnp.cos(ang)
    s = jnp.sin(ang)
    s = jnp.where(jnp.bitwise_and(lane, ROT_DIM - 1) < ROT_HALF, -s, s)
    rotated = lane < ROT_DIM
    n_rows = ang.shape[0]
    for k in range(ROPE_PACK):
        shift = (LANES - k * ROT_DIM) % LANES
        ck = jnp.where(rotated, pltpu.roll(c, shift, 1) if shift else c, 1.0)
        sk = jnp.where(rotated, pltpu.roll(s, shift, 1) if shift else s, 0.0)
        rows = pl.ds(k, n_rows, stride=ROPE_PACK)
        cos_ref[0, rows, :] = ck * Q_SCALE
        sin_ref[0, rows, :] = sk * Q_SCALE
        cos_ref[1, rows, :] = ck
        sin_ref[1, rows, :] = sk


def _rope_tables(positions, casts):
    inv_freq = ROPE_THETA ** (-jnp.arange(0, ROT_DIM, 2, dtype=F32) / ROT_DIM)
    freq_row = jnp.tile(jnp.concatenate([inv_freq, inv_freq]), ROPE_PACK).reshape(1, LANES)
    pos = jnp.repeat(positions.reshape(TOKENS // ROPE_PACK, ROPE_PACK), ROT_DIM, axis=1)
    return _call(
        _rope_table_kernel,
        name="rope_tables",
        grid=(TOKENS // ROPE_TM,),
        in_specs=[pl.BlockSpec((ROPE_TM // ROPE_PACK, LANES), lambda i: (i, 0)),
                  pl.BlockSpec((1, LANES), lambda i: (0, 0))],
        out_specs=[pl.BlockSpec((2, ROPE_TM, LANES), lambda i: (0, i, 0)),
                   pl.BlockSpec((2, ROPE_TM, LANES), lambda i: (0, i, 0))],
        out_shape=[jax.ShapeDtypeStruct((2, TOKENS, LANES), F32)] * 2,
        args=(pos, freq_row),
        casts=casts)


def _normalize_rows(x_ref, g_ref, xn_ref):
    g = g_ref[...]
    for r0 in range(0, x_ref.shape[0], NORM_ROWS):
        x = x_ref[r0:r0 + NORM_ROWS, :]
        xn_ref[r0:r0 + NORM_ROWS, :] = (x * _rms_scale(x) * g).astype(BF16)


def _add_normalized_rows(h_ref, m_ref, g_ref, o_ref):
    g = g_ref[...]
    for r0 in range(0, h_ref.shape[0], NORM_ROWS):
        rows = slice(r0, r0 + NORM_ROWS)
        m = m_ref[rows, :]
        o_ref[rows, :] = h_ref[rows, :] + m * _rms_scale(m) * g


def _plain_proj_kernel(x_ref, g_ref, w_ref, cs_ref, o_ref, xn_ref, *, rider):
    @pl.when(pl.program_id(1) == 0)
    def _():
        _normalize_rows(x_ref, g_ref, xn_ref)

    rider()
    acc = jnp.dot(xn_ref[...], w_ref[...], preferred_element_type=F32)
    o_ref[...] = (acc * cs_ref[...]).astype(o_ref.dtype)


def _plain_proj(h, g, w, col_scale, casts):
    n_tiles = len(PLAIN_W_TILES)
    first_gap = PLAIN_W_TILES.index(5)

    def w_map(i, j):
        return (0, jnp.where(j < first_gap, j, j + (5 - first_gap)))

    return _call(
        _plain_proj_kernel,
        name="even_plain_proj",
        grid=(TOKENS // PROJ_TM, n_tiles),
        in_specs=[pl.BlockSpec((PROJ_TM, D_MODEL), lambda i, j: (i, 0)),
                  pl.BlockSpec((1, D_MODEL), lambda i, j: (0, 0)),
                  pl.BlockSpec((D_MODEL, PROJ_TN), w_map),
                  pl.BlockSpec((1, PROJ_TN), lambda i, j: (0, j))],
        out_specs=[pl.BlockSpec((PROJ_TM, PROJ_TN), lambda i, j: (i, j)),
                   pl.BlockSpec((PROJ_TM, D_MODEL), lambda i, j: (i, 0))],
        out_shape=[jax.ShapeDtypeStruct((TOKENS, n_tiles * PROJ_TN), BF16),
                   jax.ShapeDtypeStruct((TOKENS, D_MODEL), BF16)],
        args=(h, g, w, col_scale),
        casts=casts)


def _rope_proj_kernel(xn_ref, w_ref, cos_ref, sin_ref, o_ref, *, rider):
    rider()
    acc = jnp.dot(xn_ref[...], w_ref[...], preferred_element_type=F32)
    c = cos_ref[...]
    s = sin_ref[...]
    first = lax.broadcasted_iota(jnp.int32, c.shape, 1) < ROT_HALF
    for g in range(PROJ_TN // LANES):
        a = acc[:, g * LANES:(g + 1) * LANES]
        swapped = jnp.where(first, pltpu.roll(a, LANES - ROT_HALF, 1), pltpu.roll(a, ROT_HALF, 1))
        o_ref[:, g * LANES:(g + 1) * LANES] = (a * c + swapped * s).astype(o_ref.dtype)


def _rope_proj(xn, w, cos_t, sin_t, casts):
    return _call(
        _rope_proj_kernel,
        name="even_rope_proj",
        grid=(TOKENS // PROJ_TM, 2),
        in_specs=[pl.BlockSpec((PROJ_TM, D_MODEL), lambda i, j: (i, 0)),
                  pl.BlockSpec((D_MODEL, PROJ_TN), lambda i, j: (0, ROPE_W_TILE0 + j)),
                  pl.BlockSpec((None, PROJ_TM, LANES), lambda i, j: (j, i, 0)),
                  pl.BlockSpec((None, PROJ_TM, LANES), lambda i, j: (j, i, 0))],
        out_specs=[pl.BlockSpec((PROJ_TM, PROJ_TN), lambda i, j: (i, j))],
        out_shape=[jax.ShapeDtypeStruct((TOKENS, 2 * PROJ_TN), BF16)],
        args=(xn, w, cos_t, sin_t),
        casts=casts)


def _gelu_tanh(x):
    c = -2.0 * math.sqrt(2.0 / math.pi) * LOG2E
    return x / (1.0 + jnp.exp2(x * (c + (0.044715 * c) * (x * x))))


def _odd_proj_kernel(x_ref, g_ref, w_ref, o_ref, xn_ref, *, rider):
    @pl.when(pl.program_id(1) == 0)
    def _():
        _normalize_rows(x_ref, g_ref, xn_ref)

    rider()
    acc = jnp.dot(xn_ref[...], w_ref[...], preferred_element_type=F32)
    o_ref[...] = _gelu_tanh(acc).astype(o_ref.dtype)


def _odd_proj(h, g, w, casts):
    n = w.shape[1]
    return _call(
        _odd_proj_kernel,
        name="odd_in_proj",
        grid=(TOKENS // PROJ_TM, n // PROJ_TN),
        in_specs=[pl.BlockSpec((PROJ_TM, D_MODEL), lambda i, j: (i, 0)),
                  pl.BlockSpec((1, D_MODEL), lambda i, j: (0, 0)),
                  pl.BlockSpec((D_MODEL, PROJ_TN), lambda i, j: (0, j))],
        out_specs=[pl.BlockSpec((PROJ_TM, PROJ_TN), lambda i, j: (i, j))],
        out_shape=[jax.ShapeDtypeStruct((TOKENS, n), BF16)],
        scratch_shapes=[pltpu.VMEM((PROJ_TM, D_MODEL), BF16)],
        args=(h, g, w),
        casts=casts)


def _store_transposed_values(v_ref, vt_ref):
    for jb in range(SEQ // ATT_BK):
        vt_ref[jb] = v_ref[jb * ATT_BK:(jb + 1) * ATT_BK, :].astype(F32).T.astype(BF16)


def _store_transposed_queries(q_ref, qt_ref):
    qt_ref[...] = q_ref[...].astype(F32).T.astype(BF16)


def _scores_t(k_ref, qt_ref, j, cols):
    kj = k_ref[pl.ds(j * ATT_BK, ATT_BK), cols]
    return jnp.dot(kj, qt_ref[cols, :], preferred_element_type=F32)


def _sb_attn_kernel(q_ref, k_ref, v_ref, o_ref, vt_ref, acc_ref, carry_ref, qt_ref, *, rider):
    i = pl.program_id(2)

    @pl.when(i == 0)
    def _():
        _store_transposed_values(v_ref, vt_ref)

    rider()
    _store_transposed_queries(q_ref, qt_ref)
    key = lax.broadcasted_iota(jnp.int32, (ATT_BK, ATT_TQ), 0)
    qry = lax.broadcasted_iota(jnp.int32, (ATT_BK, ATT_TQ), 1)
    causal = key < qry
    later_key = lax.broadcasted_iota(jnp.int32, (ATT_BK, ATT_BK), 1)
    this_key = lax.broadcasted_iota(jnp.int32, (ATT_BK, ATT_BK), 0)
    tri = (later_key > this_key).astype(BF16)

    heads = range(SB_NH)

    def head_cols(h):
        return slice(h * HEAD_DIM, (h + 1) * HEAD_DIM)

    def sweep(visits, carries):
        zs = [[_scores_t(k_ref, qt_ref, j, head_cols(h)) for h in heads] for j, _ in visits]
        log_betas, log_keeps = [], []
        for (_, diag), z_row in zip(visits, zs):
            beta_row, keep_row = [], []
            for z in z_row:
                soft = jnp.log2(1.0 + jnp.exp2(-jnp.abs(z)))
                log_beta = jnp.minimum(z, 0.0) - soft
                log_keep = log_beta - z
                if diag:
                    log_keep = jnp.where(causal, log_keep, 0.0)
                beta_row.append(log_beta)
                keep_row.append(log_keep)
            log_betas.append(beta_row)
            log_keeps.append(keep_row)
        laters = [[jnp.dot(tri, lk.astype(BF16), preferred_element_type=F32) for lk in keep_row]
                  for keep_row in log_keeps]
        total = None
        for b, (j, diag) in enumerate(visits):
            ws = []
            for h in heads:
                arg = log_betas[b][h] + laters[b][h]
                if carries is not None:
                    arg = arg + carries[h]
                w = jnp.exp2(arg)
                if diag:
                    w = jnp.where(causal, w, 0.0)
                ws.append(w.astype(BF16))
            contribs = [jnp.dot(vt_ref[j, head_cols(h), :], ws[h], preferred_element_type=F32)
                        for h in heads]
            total = contribs if total is None else [total[h] + contribs[h] for h in heads]
            colsums = [laters[b][h][0:1, :] + log_keeps[b][h][0:1, :] for h in heads]
            carries = colsums if carries is None else [carries[h] + colsums[h] for h in heads]
        return total, carries

    def any_weight_left(carries):
        worst = functools.reduce(jnp.maximum, carries)
        return (jnp.max(worst) > EXP2_ZERO_BELOW).astype(jnp.int32)

    def start(visits):
        contribs, carries = sweep(visits, None)
        for h in heads:
            acc_ref[h] = contribs[h]
            carry_ref[h] = carries[h]

    pl.when(i == 0)(functools.partial(start, [(i, True)]))
    pl.when(i > 0)(functools.partial(start, [(i, True), (i - 1, False)]))

    def cond(state):
        j, more = state
        return jnp.logical_and(j >= 0, more > 0)

    def body(state):
        j, _ = state
        contribs, carries = sweep([(j, False)], [carry_ref[h] for h in heads])
        for h in heads:
            acc_ref[h] += contribs[h]
            carry_ref[h] = carries[h]
        return j - 1, any_weight_left(carries)

    lax.while_loop(cond, body, (i - 2, any_weight_left([carry_ref[h] for h in heads])))
    for h in range(SB_NH):
        o_ref[:, h * HEAD_DIM:(h + 1) * HEAD_DIM] = acc_ref[h].T.astype(o_ref.dtype)


def _sb_attention(proj3, casts):
    width = SB_NH * HEAD_DIM
    groups = SB_HEADS // SB_NH
    return _call(
        _sb_attn_kernel,
        name="sb_attention",
        grid=(BATCH, groups, SEQ // ATT_TQ),
        in_specs=[pl.BlockSpec((None, ATT_TQ, width), lambda b, g, i: (b, i, g)),
                  pl.BlockSpec((None, SEQ, width), lambda b, g, i: (b, 0, groups + g)),
                  pl.BlockSpec((None, SEQ, width), lambda b, g, i: (b, 0, 2 * groups + g))],
        out_specs=[pl.BlockSpec((None, ATT_TQ, width), lambda b, g, i: (b, i, g))],
        out_shape=[jax.ShapeDtypeStruct((BATCH, SEQ, SB_WIDTH), BF16)],
        scratch_shapes=[pltpu.VMEM((SEQ // ATT_BK, width, ATT_BK), BF16),
                        pltpu.VMEM((SB_NH, HEAD_DIM, ATT_TQ), F32),
                        pltpu.VMEM((SB_NH, 1, ATT_TQ), F32),
                        pltpu.VMEM((width, ATT_TQ), BF16)],
        args=(proj3, proj3, proj3),
        casts=casts)


def _diff_attn_kernel(lam_ref, g_ref, q_ref, k_ref, v_ref, o_ref, vt_ref, acc_ref, m_ref, l_ref,
                      z_ref, qt_ref, *, lambda_init, rider):
    i = pl.program_id(1)

    @pl.when(i == 0)
    def _():
        _store_transposed_values(v_ref, vt_ref)

    rider()
    _store_transposed_queries(q_ref, qt_ref)
    key = lax.broadcasted_iota(jnp.int32, (ATT_BK, ATT_TQ), 0)
    qry = lax.broadcasted_iota(jnp.int32, (ATT_BK, ATT_TQ), 1)
    visible = _chunk_of(key) <= _chunk_of(qry)

    def head_cols(c):
        return slice(c * HEAD_DIM, (c + 1) * HEAD_DIM)

    def value_rows(c):
        return slice((c // 2) * DIFF_V_DIM, (c // 2 + 1) * DIFF_V_DIM)

    maps = range(DIFF_MAPS)

    def scores_all(j):
        return [_scores_t(k_ref, qt_ref, j, head_cols(c)) for c in maps]

    def absorb(j):
        ps, alphas = [], []
        for c in maps:
            z = z_ref[c]
            m_old = m_ref[c]
            m_new = jnp.maximum(m_old, jnp.max(z, axis=0, keepdims=True))
            alpha = jnp.exp2(m_old - m_new)
            p = jnp.exp2(z - m_new)
            m_ref[c] = m_new
            l_ref[c] = l_ref[c] * alpha + jnp.sum(p, axis=0, keepdims=True)
            ps.append(p.astype(BF16))
            alphas.append(alpha)
        pvs = [jnp.dot(vt_ref[j, value_rows(c), :], ps[c], preferred_element_type=F32)
               for c in maps]
        for c in maps:
            acc_ref[c] = acc_ref[c] * alphas[c] + pvs[c]

    for c in maps:
        m_ref[c] = jnp.full((1, ATT_TQ), -jnp.inf, F32)
        l_ref[c] = jnp.zeros((1, ATT_TQ), F32)
        acc_ref[c] = jnp.zeros((DIFF_V_DIM, ATT_TQ), F32)
    for c, z in enumerate(scores_all(i)):
        z_ref[c] = jnp.where(visible, z, -jnp.inf)

    def body(jj, unused):
        j = i - jj
        zs_next = scores_all(j - 1)
        absorb(j)
        for c in maps:
            z_ref[c] = zs_next[c]
        return unused

    lax.fori_loop(0, i, body, 0)
    absorb(0)

    lam_rows = lam_ref[...]
    s1 = jnp.sum(lam_rows[0:1, :] * lam_rows[1:2, :], axis=-1, keepdims=True)
    s2 = jnp.sum(lam_rows[2:3, :] * lam_rows[3:4, :], axis=-1, keepdims=True)
    lam = jnp.exp(s1) - jnp.exp(s2) + lambda_init
    gain = g_ref[...]
    for h in range(DIFF_HEADS):
        o = acc_ref[2 * h] / l_ref[2 * h] - lam * (acc_ref[2 * h + 1] / l_ref[2 * h + 1])
        r = lax.rsqrt(jnp.mean(o * o, axis=0, keepdims=True) + EPS)
        o = o * r * gain * (1.0 - lambda_init)
        o_ref[:, h * DIFF_V_DIM:(h + 1) * DIFF_V_DIM] = o.T.astype(o_ref.dtype)


def _diff_attention(qk3, proj3, lam_rows, subln_col, lambda_init, casts):
    v_tile = PLAIN_W_TILES.index(5)
    return _call(
        functools.partial(_diff_attn_kernel, lambda_init=lambda_init),
        name="diff_attention",
        grid=(BATCH, SEQ // ATT_TQ),
        in_specs=[pl.BlockSpec((4, HEAD_DIM), lambda b, i: (0, 0)),
                  pl.BlockSpec((DIFF_V_DIM, 1), lambda b, i: (0, 0)),
                  pl.BlockSpec((None, ATT_TQ, DIFF_QK_WIDTH), lambda b, i: (b, i, 0)),
                  pl.BlockSpec((None, SEQ, DIFF_QK_WIDTH), lambda b, i: (b, 0, 1)),
                  pl.BlockSpec((None, SEQ, DIFF_V_WIDTH), lambda b, i: (b, 0, v_tile))],
        out_specs=[pl.BlockSpec((None, ATT_TQ, DIFF_V_WIDTH), lambda b, i: (b, i, 0))],
        out_shape=[jax.ShapeDtypeStruct((BATCH, SEQ, DIFF_V_WIDTH), BF16)],
        scratch_shapes=[pltpu.VMEM((SEQ // ATT_BK, DIFF_V_WIDTH, ATT_BK), BF16),
                        pltpu.VMEM((DIFF_MAPS, DIFF_V_DIM, ATT_TQ), F32),
                        pltpu.VMEM((DIFF_MAPS, 1, ATT_TQ), F32),
                        pltpu.VMEM((DIFF_MAPS, 1, ATT_TQ), F32),
                        pltpu.VMEM((DIFF_MAPS, ATT_BK, ATT_TQ), F32),
                        pltpu.VMEM((DIFF_QK_WIDTH, ATT_TQ), BF16)],
        args=(lam_rows, subln_col, qk3, qk3, proj3),
        casts=casts)


def _out_proj_kernel(*refs, n_in, rider):
    a_refs = refs[:n_in]
    w_refs = refs[n_in:2 * n_in]
    g_ref, h_ref, o_ref = refs[2 * n_in:]
    rider()
    m = jnp.dot(a_refs[0][...], w_refs[0][...], preferred_element_type=F32)
    for a_ref, w_ref in zip(a_refs[1:], w_refs[1:]):
        m = m + jnp.dot(a_ref[...], w_ref[...], preferred_element_type=F32)
    o_ref[...] = m
    _add_normalized_rows(h_ref, o_ref, g_ref, o_ref)


def _out_proj(acts, w, g, h, name, casts=()):
    n_in = len(acts)
    k = w.shape[0] // n_in
    in_specs = ([pl.BlockSpec((OUT_TM, k), lambda i: (i, 0)) for _ in acts]
                + [pl.BlockSpec((k, D_MODEL), functools.partial(lambda i, r: (r, 0), r=r))
                   for r in range(n_in)]
                + [pl.BlockSpec((1, D_MODEL), lambda i: (0, 0)),
                   pl.BlockSpec((OUT_TM, D_MODEL), lambda i: (i, 0))])
    return _call(
        functools.partial(_out_proj_kernel, n_in=n_in),
        name=name,
        grid=(TOKENS // OUT_TM,),
        in_specs=in_specs,
        out_specs=[pl.BlockSpec((OUT_TM, D_MODEL), lambda i: (i, 0))],
        out_shape=[jax.ShapeDtypeStruct((TOKENS, D_MODEL), F32)],
        args=(*acts, *([w] * n_in), g, h),
        casts=casts)


def _ffn_kernel(h_ref, g1_ref, w1_ref, w2_ref, g2_ref, o_ref, hn_ref, *, rider):
    f = pl.program_id(1)

    def partial_product():
        rider()
        a = jnp.maximum(jnp.dot(hn_ref[...], w1_ref[...], preferred_element_type=F32), 0.0)
        return jnp.dot((a * a).astype(BF16), w2_ref[...], preferred_element_type=F32)

    @pl.when(f == 0)
    def _():
        _normalize_rows(h_ref, g1_ref, hn_ref)
        o_ref[...] = partial_product()

    @pl.when(f != 0)
    def _():
        o_ref[...] += partial_product()

    @pl.when(f == pl.num_programs(1) - 1)
    def _():
        _add_normalized_rows(h_ref, o_ref, g2_ref, o_ref)


def _ffn(h, g1, w1, w2, g2, casts):
    return _call(
        _ffn_kernel,
        name="ffn",
        grid=(TOKENS // FFN_TM, D_FF // FFN_TF),
        in_specs=[pl.BlockSpec((FFN_TM, D_MODEL), lambda i, f: (i, 0)),
                  pl.BlockSpec((1, D_MODEL), lambda i, f: (0, 0)),
                  pl.BlockSpec((D_MODEL, FFN_TF), lambda i, f: (0, f)),
                  pl.BlockSpec((FFN_TF, D_MODEL), lambda i, f: (f, 0)),
                  pl.BlockSpec((1, D_MODEL), lambda i, f: (0, 0))],
        out_specs=[pl.BlockSpec((FFN_TM, D_MODEL), lambda i, f: (i, 0))],
        out_shape=[jax.ShapeDtypeStruct((TOKENS, D_MODEL), F32)],
        scratch_shapes=[pltpu.VMEM((FFN_TM, D_MODEL), BF16)],
        args=(h, g1, w1, w2, g2),
        casts=casts)


def _ple_kernel(h_ref, p_ref, wg_ref, wp_ref, g_ref, o_ref, *, rider):
    rider()
    z = jnp.dot(h_ref[...].astype(BF16), wg_ref[...], preferred_element_type=F32)
    e = jnp.dot(p_ref[...].astype(BF16), wp_ref[...], preferred_element_type=F32)
    o_ref[...] = e / (1.0 + jnp.exp2(z * (-LOG2E)))
    _add_normalized_rows(h_ref, o_ref, g_ref, o_ref)


def _ple(h, p_stack, layer, wg, wp, g, casts=()):
    return _call(
        _ple_kernel,
        name="ple",
        grid=(TOKENS // PLE_TM,),
        in_specs=[pl.BlockSpec((PLE_TM, D_MODEL), lambda i: (i, 0)),
                  pl.BlockSpec((None, PLE_TM, PLE_DIM), lambda i: (layer, i, 0)),
                  pl.BlockSpec((D_MODEL, D_MODEL), lambda i: (0, 0)),
                  pl.BlockSpec((PLE_DIM, D_MODEL), lambda i: (0, 0)),
                  pl.BlockSpec((1, D_MODEL), lambda i: (0, 0))],
        out_specs=[pl.BlockSpec((PLE_TM, D_MODEL), lambda i: (i, 0))],
        out_shape=[jax.ShapeDtypeStruct((TOKENS, D_MODEL), F32)],
        args=(h, p_stack, wg, wp, g),
        casts=casts)


def _sg_kernel(u_ref, v_ref, lng_ref, lnb_ref, ws_ref, bst_ref, y_ref, vn_ref, *, rider):
    rider()
    ln_g = lng_ref[...]
    ln_b = lnb_ref[...]
    for r0 in range(0, SG_TM, NORM_ROWS):
        rows = slice(r0, r0 + NORM_ROWS)
        v = v_ref[rows, :].astype(F32)
        vc = v - jnp.mean(v, axis=-1, keepdims=True)
        inv = lax.rsqrt(jnp.mean(vc * vc, axis=-1, keepdims=True) + EPS)
        vn_ref[rows, :] = (vc * inv * ln_g + ln_b).astype(BF16)

    n_blk = SG_TM // SG_BLOCK
    t_idx = lax.broadcasted_iota(jnp.int32, (SG_BLOCK, SG_BLOCK), 0)
    s_idx = lax.broadcasted_iota(jnp.int32, (SG_BLOCK, SG_BLOCK), 1)
    visible = _chunk_of(s_idx) <= _chunk_of(t_idx)
    bst = bst_ref[...]
    for g in range(SG_GROUPS):
        cols = slice(g * SG_GROUP_DIM, (g + 1) * SG_GROUP_DIM)
        w = jnp.where(visible, ws_ref[g], 0.0).astype(BF16)
        rhs = jnp.concatenate(
            [vn_ref[n * SG_BLOCK:(n + 1) * SG_BLOCK, cols] for n in range(n_blk)], axis=1)
        mixed = jnp.dot(w, rhs, preferred_element_type=F32) + bst[:, g:g + 1]
        for n in range(n_blk):
            rows = slice(n * SG_BLOCK, (n + 1) * SG_BLOCK)
            u = u_ref[rows, cols].astype(F32)
            y_ref[rows, cols] = (u * mixed[:, n * SG_BLOCK:(n + 1) * SG_BLOCK]).astype(y_ref.dtype)


def _spatial_gate(uv, ln_g, ln_b, w_s_stack, layer, b_s_t, casts):
    return _call(
        _sg_kernel,
        name="spatial_gate",
        grid=(TOKENS // SG_TM,),
        in_specs=[pl.BlockSpec((SG_TM, SG_WIDTH), lambda i: (i, 0)),
                  pl.BlockSpec((SG_TM, SG_WIDTH), lambda i: (i, 1)),
                  pl.BlockSpec((1, SG_WIDTH), lambda i: (0, 0)),
                  pl.BlockSpec((1, SG_WIDTH), lambda i: (0, 0)),
                  pl.BlockSpec((None, SG_GROUPS, SG_BLOCK, SG_BLOCK), lambda i: (layer, 0, 0, 0)),
                  pl.BlockSpec((SG_BLOCK, SG_GROUPS), lambda i: (0, 0))],
        out_specs=[pl.BlockSpec((SG_TM, SG_WIDTH), lambda i: (i, 0))],
        out_shape=[jax.ShapeDtypeStruct((TOKENS, SG_WIDTH), BF16)],
        scratch_shapes=[pltpu.VMEM((SG_TM, SG_WIDTH), BF16)],
        args=(uv, uv, ln_g, ln_b, w_s_stack, b_s_t),
        casts=casts)


def _row(v):
    return v.reshape(1, -1)


def kernel(x, p, positions, ev_norm_pre, ev_w_in, ev_lam_q1, ev_lam_k1, ev_lam_q2, ev_lam_k2,
           ev_subln, ev_w_out, ev_norm_post, od_norm_pre, od_w_in, od_ln_g, od_ln_b, od_w_s,
           od_b_s, od_w_out, od_norm_post, ffn_norm_pre, ffn_w1, ffn_w2, ffn_norm_post,
           ple_w_proj, ple_w_gate, ple_norm):
    assert DEPTH == 2, "the cast schedule below is written for one even and one odd layer"
    h = x.reshape(TOKENS, D_MODEL)
    p_stack = p.reshape(DEPTH, TOKENS, PLE_DIM)
    (cos_t, sin_t), (w_in0,) = _rope_tables(positions, casts=[_Cast(ev_w_in, 0)])
    sb_q_scale = jnp.concatenate([jnp.full((1, SB_WIDTH), Q_SCALE, F32),
                                  jnp.ones((1, (len(PLAIN_W_TILES) - 1) * PROJ_TN), F32)], axis=1)

    lambda_init = 0.8 - 0.6 * math.exp(-0.3 * 0)
    (proj, xn), (w_out0,) = _plain_proj(h, _row(ev_norm_pre[0]), w_in0, sb_q_scale,
                                        casts=[_Cast(ev_w_out, 0)])
    (qk,), (ffn_w2_0,) = _rope_proj(xn, w_in0, cos_t, sin_t, casts=[_Cast(ffn_w2, 0)])
    proj3 = proj.reshape(BATCH, SEQ, len(PLAIN_W_TILES) * PROJ_TN)
    qk3 = qk.reshape(BATCH, SEQ, 2 * DIFF_QK_WIDTH)
    (sb_o,), (ffn_w1_0,) = _sb_attention(proj3, casts=[_Cast(ffn_w1, 0)])
    lam_rows = jnp.stack([ev_lam_q1[0], ev_lam_k1[0], ev_lam_q2[0], ev_lam_k2[0]])
    (df_o,), (gate_0, pproj_0) = _diff_attention(
        qk3, proj3, lam_rows, ev_subln[0].reshape(DIFF_V_DIM, 1), lambda_init,
        casts=[_Cast(ple_w_gate, 0), _Cast(ple_w_proj, 0)])
    (h,), _ = _out_proj([sb_o.reshape(TOKENS, SB_WIDTH), df_o.reshape(TOKENS, DIFF_V_WIDTH)],
                        w_out0, _row(ev_norm_post[0]), h, "even_out_proj")
    (h,), _ = _ffn(h, _row(ffn_norm_pre[0]), ffn_w1_0, ffn_w2_0, _row(ffn_norm_post[0]), casts=[])
    (h,), (od_in, od_out) = _ple(h, p_stack, 0, gate_0, pproj_0, _row(ple_norm[0]),
                                 casts=[_Cast(od_w_in, 0), _Cast(od_w_out, 0)])

    (uv,), (ffn_w1_1,) = _odd_proj(h, _row(od_norm_pre[0]), od_in, casts=[_Cast(ffn_w1, 1)])
    (y,), (gate_1, pproj_1) = _spatial_gate(
        uv, _row(od_ln_g[0]), _row(od_ln_b[0]), od_w_s, 0, od_b_s[0].T,
        casts=[_Cast(ple_w_gate, 1), _Cast(ple_w_proj, 1)])
    (h,), (ffn_w2_1,) = _out_proj([y], od_out, _row(od_norm_post[0]), h, "odd_out_proj",
                                  casts=[_Cast(ffn_w2, 1)])
    (h,), _ = _ffn(h, _row(ffn_norm_pre[1]), ffn_w1_1, ffn_w2_1, _row(ffn_norm_post[1]), casts=[])
    (h,), _ = _ple(h, p_stack, 1, gate_1, pproj_1, _row(ple_norm[1]))
    return h.reshape(BATCH, SEQ, D_MODEL)
```

```python
import functools
import math
from typing import NamedTuple

import jax
import jax.numpy as jnp
from jax import lax
from jax.experimental import pallas as pl
from jax.experimental.pallas import tpu as pltpu

D_MODEL = 2048
BATCH = 8
SEQ = 2048
DEPTH = 2
TOKENS = BATCH * SEQ

CHUNK = 64
HEAD_DIM = 128
SB_HEADS = 8
DIFF_HEADS = 4
DIFF_MAPS = 2 * DIFF_HEADS
DIFF_V_DIM = 2 * HEAD_DIM
ROT_DIM = HEAD_DIM // 4
ROT_HALF = ROT_DIM // 2
LANES = 128
ROPE_PACK = LANES // ROT_DIM
ROPE_THETA = 500000.0
SG_BLOCK = 128
SG_GROUPS = 16
SG_GROUP_DIM = 128
SG_WIDTH = SG_GROUPS * SG_GROUP_DIM
D_FF = 4 * D_MODEL
PLE_DIM = 256
SB_WIDTH = SB_HEADS * HEAD_DIM
DIFF_QK_WIDTH = DIFF_HEADS * 2 * HEAD_DIM
DIFF_V_WIDTH = DIFF_HEADS * DIFF_V_DIM
EVEN_IN_WIDTH = 3 * SB_WIDTH + 2 * DIFF_QK_WIDTH + DIFF_V_WIDTH
EPS = 1e-6
LOG2E = math.log2(math.e)
Q_SCALE = HEAD_DIM ** -0.5 * LOG2E
EXP2_ZERO_BELOW = -151.0

BF16_SUBLANES = 16
VMEM_LIMIT_BYTES = 56 * 1024 * 1024

F32 = jnp.float32
BF16 = jnp.bfloat16

PROJ_TM = 1024
PROJ_TN = 1024
ATT_TQ = 256
ATT_BK = 256
SB_NH = 8
OUT_TM = 512
FFN_TM = 512
FFN_TF = 1024
PLE_TM = 512
SG_TM = 1024
ROPE_TM = 2048
NORM_ROWS = 16

PLAIN_W_TILES = (0, 1, 2, 5)
ROPE_W_TILE0 = 3


def _chunk_of(idx):
    return jnp.right_shift(idx, CHUNK.bit_length() - 1)


def _rms_scale(x):
    return lax.rsqrt(jnp.mean(x * x, axis=-1, keepdims=True) + EPS)


class _Cast(NamedTuple):
    stack: jax.Array
    layer: int


def _no_rider():
    pass


def _call(body, *, name, grid, in_specs, out_specs, out_shape, args, scratch_shapes=(), casts=()):
    steps = math.prod(grid)
    n_in, n_out, n_cast = len(in_specs), len(out_specs), len(casts)

    def linear_step(*idx):
        step = idx[0]
        for extent, k in zip(grid[1:], idx[1:]):
            step = step * extent + k
        return step

    cast_in, cast_out, cast_shape = [], [], []
    for job in casts:
        _, rows, cols = job.stack.shape
        assert rows % BF16_SUBLANES == 0
        n_blocks = math.gcd(steps, rows // BF16_SUBLANES)
        blk_rows = rows // n_blocks
        repeat = steps // n_blocks
        cast_in.append(pl.BlockSpec(
            (None, blk_rows, cols),
            functools.partial(lambda *idx, layer, repeat: (layer, linear_step(*idx) // repeat, 0),
                              layer=job.layer, repeat=repeat)))
        cast_out.append(pl.BlockSpec(
            (blk_rows, cols),
            functools.partial(lambda *idx, repeat: (linear_step(*idx) // repeat, 0), repeat=repeat)))
        cast_shape.append(jax.ShapeDtypeStruct((rows, cols), BF16))

    def wrapped(*refs):
        ins = refs[:n_in]
        srcs = refs[n_in:n_in + n_cast]
        outs = refs[n_in + n_cast:n_in + n_cast + n_out]
        dsts = refs[n_in + n_cast + n_out:n_in + 2 * n_cast + n_out]
        scratch = refs[n_in + 2 * n_cast + n_out:]

        def rider():
            for src, dst in zip(srcs, dsts):
                dst[...] = src[...].astype(dst.dtype)

        body(*ins, *outs, *scratch, rider=rider if n_cast else _no_rider)

    results = pl.pallas_call(
        wrapped,
        grid=grid,
        in_specs=list(in_specs) + cast_in,
        out_specs=list(out_specs) + cast_out,
        out_shape=list(out_shape) + cast_shape,
        scratch_shapes=list(scratch_shapes),
        compiler_params=pltpu.CompilerParams(dimension_semantics=("arbitrary",) * len(grid),
                                             vmem_limit_bytes=VMEM_LIMIT_BYTES),
        name=name,
    )(*args, *[job.stack for job in casts])
    return results[:n_out], results[n_out:]


def _rope_table_kernel(pos_ref, freq_ref, cos_ref, sin_ref, *, rider):
    rider()
    ang = pos_ref[...].astype(F32) * freq_ref[...]
    lane = lax.broadcasted_iota(jnp.int32, ang.shape, 1)
    c = jnp.cos(ang)
    s = jnp.sin(ang)
    s = jnp.where(jnp.bitwise_and(lane, ROT_DIM - 1) < ROT_HALF, -s, s)
    rotated = lane < ROT_DIM
    n_rows = ang.shape[0]
    for k in range(ROPE_PACK):
        shift = (LANES - k * ROT_DIM) % LANES
        ck = jnp.where(rotated, pltpu.roll(c, shift, 1) if shift else c, 1.0)
        sk = jnp.where(rotated, pltpu.roll(s, shift, 1) if shift else s, 0.0)
        rows = pl.ds(k, n_rows, stride=ROPE_PACK)
        cos_ref[0, rows, :] = ck * Q_SCALE
        sin_ref[0, rows, :] = sk * Q_SCALE
        cos_ref[1, rows, :] = ck
        sin_ref[1, rows, :] = sk


def _rope_tables(positions, casts):
    inv_freq = ROPE_THETA ** (-jnp.arange(0, ROT_DIM, 2, dtype=F32) / ROT_DIM)
    freq_row = jnp.tile(jnp.concatenate([inv_freq, inv_freq]), ROPE_PACK).reshape(1, LANES)
    pos = jnp.repeat(positions.reshape(TOKENS // ROPE_PACK, ROPE_PACK), ROT_DIM, axis=1)
    return _call(
        _rope_table_kernel,
        name="rope_tables",
        grid=(TOKENS // ROPE_TM,),
        in_specs=[pl.BlockSpec((ROPE_TM // ROPE_PACK, LANES), lambda i: (i, 0)),
                  pl.BlockSpec((1, LANES), lambda i: (0, 0))],
        out_specs=[pl.BlockSpec((2, ROPE_TM, LANES), lambda i: (0, i, 0)),
                   pl.BlockSpec((2, ROPE_TM, LANES), lambda i: (0, i, 0))],
        out_shape=[jax.ShapeDtypeStruct((2, TOKENS, LANES), F32)] * 2,
        args=(pos, freq_row),
        casts=casts)


def _normalize_rows(x_ref, g_ref, xn_ref):
    g = g_ref[...]
    for r0 in range(0, x_ref.shape[0], NORM_ROWS):
        x = x_ref[r0:r0 + NORM_ROWS, :]
        xn_ref[r0:r0 + NORM_ROWS, :] = (x * _rms_scale(x) * g).astype(BF16)


def _add_normalized_rows(h_ref, m_ref, g_ref, o_ref):
    g = g_ref[...]
    for r0 in range(0, h_ref.shape[0], NORM_ROWS):
        rows = slice(r0, r0 + NORM_ROWS)
        m = m_ref[rows, :]
        o_ref[rows, :] = h_ref[rows, :] + m * _rms_scale(m) * g


def _plain_proj_kernel(x_ref, g_ref, w_ref, cs_ref, o_ref, xn_ref, *, rider):
    def project():
        rider()
        acc = jnp.dot(xn_ref[...], w_ref[...], preferred_element_type=F32)
        o_ref[...] = (acc * cs_ref[...]).astype(o_ref.dtype)

    @pl.when(pl.program_id(1) == 0)
    def _():
        _normalize_rows(x_ref, g_ref, xn_ref)
        project()

    pl.when(pl.program_id(1) != 0)(project)


def _plain_proj(h, g, w, col_scale, casts):
    n_tiles = len(PLAIN_W_TILES)
    first_gap = PLAIN_W_TILES.index(5)

    def w_map(i, j):
        return (0, jnp.where(j < first_gap, j, j + (5 - first_gap)))

    return _call(
        _plain_proj_kernel,
        name="even_plain_proj",
        grid=(TOKENS // PROJ_TM, n_tiles),
        in_specs=[pl.BlockSpec((PROJ_TM, D_MODEL), lambda i, j: (i, 0)),
                  pl.BlockSpec((1, D_MODEL), lambda i, j: (0, 0)),
                  pl.BlockSpec((D_MODEL, PROJ_TN), w_map),
                  pl.BlockSpec((1, PROJ_TN), lambda i, j: (0, j))],
        out_specs=[pl.BlockSpec((PROJ_TM, PROJ_TN), lambda i, j: (i, j)),
                   pl.BlockSpec((PROJ_TM, D_MODEL), lambda i, j: (i, 0))],
        out_shape=[jax.ShapeDtypeStruct((TOKENS, n_tiles * PROJ_TN), BF16),
                   jax.ShapeDtypeStruct((TOKENS, D_MODEL), BF16)],
        args=(h, g, w, col_scale),
        casts=casts)


def _rope_proj_kernel(xn_ref, w_ref, cos_ref, sin_ref, o_ref, *, rider):
    rider()
    acc = jnp.dot(xn_ref[...], w_ref[...], preferred_element_type=F32)
    c = cos_ref[...]
    s = sin_ref[...]
    first = lax.broadcasted_iota(jnp.int32, c.shape, 1) < ROT_HALF
    for g in range(PROJ_TN // LANES):
        a = acc[:, g * LANES:(g + 1) * LANES]
        swapped = jnp.where(first, pltpu.roll(a, LANES - ROT_HALF, 1), pltpu.roll(a, ROT_HALF, 1))
        o_ref[:, g * LANES:(g + 1) * LANES] = (a * c + swapped * s).astype(o_ref.dtype)


def _rope_proj(xn, w, cos_t, sin_t, casts):
    return _call(
        _rope_proj_kernel,
        name="even_rope_proj",
        grid=(TOKENS // PROJ_TM, 2),
        in_specs=[pl.BlockSpec((PROJ_TM, D_MODEL), lambda i, j: (i, 0)),
                  pl.BlockSpec((D_MODEL, PROJ_TN), lambda i, j: (0, ROPE_W_TILE0 + j)),
                  pl.BlockSpec((None, PROJ_TM, LANES), lambda i, j: (j, i, 0)),
                  pl.BlockSpec((None, PROJ_TM, LANES), lambda i, j: (j, i, 0))],
        out_specs=[pl.BlockSpec((PROJ_TM, PROJ_TN), lambda i, j: (i, j))],
        out_shape=[jax.ShapeDtypeStruct((TOKENS, 2 * PROJ_TN), BF16)],
        args=(xn, w, cos_t, sin_t),
        casts=casts)


def _gelu_tanh(x):
    c = -2.0 * math.sqrt(2.0 / math.pi) * LOG2E
    return x / (1.0 + jnp.exp2(x * (c + (0.044715 * c) * (x * x))))


def _odd_proj_kernel(x_ref, g_ref, w_ref, o_ref, xn_ref, *, rider):
    def project():
        rider()
        acc = jnp.dot(xn_ref[...], w_ref[...], preferred_element_type=F32)
        o_ref[...] = _gelu_tanh(acc).astype(o_ref.dtype)

    @pl.when(pl.program_id(1) == 0)
    def _():
        _normalize_rows(x_ref, g_ref, xn_ref)
        project()

    pl.when(pl.program_id(1) != 0)(project)


def _odd_proj(h, g, w, casts):
    n = w.shape[1]
    return _call(
        _odd_proj_kernel,
        name="odd_in_proj",
        grid=(TOKENS // PROJ_TM, n // PROJ_TN),
        in_specs=[pl.BlockSpec((PROJ_TM, D_MODEL), lambda i, j: (i, 0)),
                  pl.BlockSpec((1, D_MODEL), lambda i, j: (0, 0)),
                  pl.BlockSpec((D_MODEL, PROJ_TN), lambda i, j: (0, j))],
        out_specs=[pl.BlockSpec((PROJ_TM, PROJ_TN), lambda i, j: (i, j))],
        out_shape=[jax.ShapeDtypeStruct((TOKENS, n), BF16)],
        scratch_shapes=[pltpu.VMEM((PROJ_TM, D_MODEL), BF16)],
        args=(h, g, w),
        casts=casts)


def _store_transposed_values(v_ref, vt_ref):
    for jb in range(SEQ // ATT_BK):
        vt_ref[jb] = v_ref[jb * ATT_BK:(jb + 1) * ATT_BK, :].astype(F32).T.astype(BF16)


def _store_transposed_queries(q_ref, qt_ref):
    qt_ref[...] = q_ref[...].astype(F32).T.astype(BF16)


def _scores_t(k_ref, qt_ref, j, cols):
    kj = k_ref[pl.ds(j * ATT_BK, ATT_BK), cols]
    return jnp.dot(kj, qt_ref[cols, :], preferred_element_type=F32)


def _sb_attn_kernel(q_ref, k_ref, v_ref, o_ref, vt_ref, acc_ref, carry_ref, qt_ref, *, rider):
    i = pl.program_id(2)

    @pl.when(i == 0)
    def _():
        _store_transposed_values(v_ref, vt_ref)

    rider()
    _store_transposed_queries(q_ref, qt_ref)
    key = lax.broadcasted_iota(jnp.int32, (ATT_BK, ATT_TQ), 0)
    qry = lax.broadcasted_iota(jnp.int32, (ATT_BK, ATT_TQ), 1)
    causal = key < qry
    later_key = lax.broadcasted_iota(jnp.int32, (ATT_BK, ATT_BK), 1)
    this_key = lax.broadcasted_iota(jnp.int32, (ATT_BK, ATT_BK), 0)
    tri = (later_key > this_key).astype(BF16)

    heads = range(SB_NH)

    def head_cols(h):
        return slice(h * HEAD_DIM, (h + 1) * HEAD_DIM)

    def sweep(visits, carries):
        zs = [[_scores_t(k_ref, qt_ref, j, head_cols(h)) for h in heads] for j, _ in visits]
        log_betas, log_keeps = [], []
        for (_, diag), z_row in zip(visits, zs):
            beta_row, keep_row = [], []
            for z in z_row:
                soft = jnp.log2(1.0 + jnp.exp2(-jnp.abs(z)))
                log_beta = jnp.minimum(z, 0.0) - soft
                log_keep = log_beta - z
                if diag:
                    log_keep = jnp.where(causal, log_keep, 0.0)
                beta_row.append(log_beta)
                keep_row.append(log_keep)
            log_betas.append(beta_row)
            log_keeps.append(keep_row)
        laters = [[jnp.dot(tri, lk.astype(BF16), preferred_element_type=F32) for lk in keep_row]
                  for keep_row in log_keeps]
        total = None
        for b, (j, diag) in enumerate(visits):
            ws = []
            for h in heads:
                arg = log_betas[b][h] + laters[b][h]
                if carries is not None:
                    arg = arg + carries[h]
                w = jnp.exp2(arg)
                if diag:
                    w = jnp.where(causal, w, 0.0)
                ws.append(w.astype(BF16))
            contribs = [jnp.dot(vt_ref[j, head_cols(h), :], ws[h], preferred_element_type=F32)
                        for h in heads]
            total = contribs if total is None else [total[h] + contribs[h] for h in heads]
            colsums = [laters[b][h][0:1, :] + log_keeps[b][h][0:1, :] for h in heads]
            carries = colsums if carries is None else [carries[h] + colsums[h] for h in heads]
        return total, carries

    def any_weight_left(carries):
        worst = functools.reduce(jnp.maximum, carries)
        return (jnp.max(worst) > EXP2_ZERO_BELOW).astype(jnp.int32)

    def start(visits):
        contribs, carries = sweep(visits, None)
        for h in heads:
            acc_ref[h] = contribs[h]
            carry_ref[h] = carries[h]

    pl.when(i == 0)(functools.partial(start, [(i, True)]))
    pl.when(i > 0)(functools.partial(start, [(i, True), (i - 1, False)]))

    def cond(state):
        j, more = state
        return jnp.logical_and(j >= 0, more > 0)

    def body(state):
        j, _ = state
        contribs, carries = sweep([(j, False)], [carry_ref[h] for h in heads])
        for h in heads:
            acc_ref[h] += contribs[h]
            carry_ref[h] = carries[h]
        return j - 1, any_weight_left(carries)

    lax.while_loop(cond, body, (i - 2, any_weight_left([carry_ref[h] for h in heads])))
    for h in range(SB_NH):
        o_ref[:, h * HEAD_DIM:(h + 1) * HEAD_DIM] = acc_ref[h].T.astype(o_ref.dtype)


def _sb_attention(proj3, casts):
    width = SB_NH * HEAD_DIM
    groups = SB_HEADS // SB_NH
    return _call(
        _sb_attn_kernel,
        name="sb_attention",
        grid=(BATCH, groups, SEQ // ATT_TQ),
        in_specs=[pl.BlockSpec((None, ATT_TQ, width), lambda b, g, i: (b, i, g)),
                  pl.BlockSpec((None, SEQ, width), lambda b, g, i: (b, 0, groups + g)),
                  pl.BlockSpec((None, SEQ, width), lambda b, g, i: (b, 0, 2 * groups + g))],
        out_specs=[pl.BlockSpec((None, ATT_TQ, width), lambda b, g, i: (b, i, g))],
        out_shape=[jax.ShapeDtypeStruct((BATCH, SEQ, SB_WIDTH), BF16)],
        scratch_shapes=[pltpu.VMEM((SEQ // ATT_BK, width, ATT_BK), BF16),
                        pltpu.VMEM((SB_NH, HEAD_DIM, ATT_TQ), F32),
                        pltpu.VMEM((SB_NH, 1, ATT_TQ), F32),
                        pltpu.VMEM((width, ATT_TQ), BF16)],
        args=(proj3, proj3, proj3),
        casts=casts)


def _diff_attn_kernel(lam_ref, g_ref, q_ref, k_ref, v_ref, o_ref, vt_ref, acc_ref, m_ref, l_ref,
                      z_ref, qt_ref, *, lambda_init, rider):
    i = pl.program_id(1)

    @pl.when(i == 0)
    def _():
        _store_transposed_values(v_ref, vt_ref)

    rider()
    _store_transposed_queries(q_ref, qt_ref)
    key = lax.broadcasted_iota(jnp.int32, (ATT_BK, ATT_TQ), 0)
    qry = lax.broadcasted_iota(jnp.int32, (ATT_BK, ATT_TQ), 1)
    visible = _chunk_of(key) <= _chunk_of(qry)

    def head_cols(c):
        return slice(c * HEAD_DIM, (c + 1) * HEAD_DIM)

    def value_rows(c):
        return slice((c // 2) * DIFF_V_DIM, (c // 2 + 1) * DIFF_V_DIM)

    maps = range(DIFF_MAPS)

    def scores_all(j):
        return [_scores_t(k_ref, qt_ref, j, head_cols(c)) for c in maps]

    def absorb(j):
        ps, alphas = [], []
        for c in maps:
            z = z_ref[c]
            m_old = m_ref[c]
            m_new = jnp.maximum(m_old, jnp.max(z, axis=0, keepdims=True))
            alpha = jnp.exp2(m_old - m_new)
            p = jnp.exp2(z - m_new)
            m_ref[c] = m_new
            l_ref[c] = l_ref[c] * alpha + jnp.sum(p, axis=0, keepdims=True)
            ps.append(p.astype(BF16))
            alphas.append(alpha)
        pvs = [jnp.dot(vt_ref[j, value_rows(c), :], ps[c], preferred_element_type=F32)
               for c in maps]
        for c in maps:
            acc_ref[c] = acc_ref[c] * alphas[c] + pvs[c]

    for c in maps:
        m_ref[c] = jnp.full((1, ATT_TQ), -jnp.inf, F32)
        l_ref[c] = jnp.zeros((1, ATT_TQ), F32)
        acc_ref[c] = jnp.zeros((DIFF_V_DIM, ATT_TQ), F32)
    for c, z in enumerate(scores_all(i)):
        z_ref[c] = jnp.where(visible, z, -jnp.inf)

    def body(jj, unused):
        j = i - jj
        zs_next = scores_all(j - 1)
        absorb(j)
        for c in maps:
            z_ref[c] = zs_next[c]
        return unused

    lax.fori_loop(0, i, body, 0)
    absorb(0)

    lam_rows = lam_ref[...]
    s1 = jnp.sum(lam_rows[0:1, :] * lam_rows[1:2, :], axis=-1, keepdims=True)
    s2 = jnp.sum(lam_rows[2:3, :] * lam_rows[3:4, :], axis=-1, keepdims=True)
    lam = jnp.exp(s1) - jnp.exp(s2) + lambda_init
    gain = g_ref[...]
    for h in range(DIFF_HEADS):
        o = acc_ref[2 * h] / l_ref[2 * h] - lam * (acc_ref[2 * h + 1] / l_ref[2 * h + 1])
        r = lax.rsqrt(jnp.mean(o * o, axis=0, keepdims=True) + EPS)
        o = o * r * gain * (1.0 - lambda_init)
        o_ref[:, h * DIFF_V_DIM:(h + 1) * DIFF_V_DIM] = o.T.astype(o_ref.dtype)


def _diff_attention(qk3, proj3, lam_rows, subln_col, lambda_init, casts):
    v_tile = PLAIN_W_TILES.index(5)
    return _call(
        functools.partial(_diff_attn_kernel, lambda_init=lambda_init),
        name="diff_attention",
        grid=(BATCH, SEQ // ATT_TQ),
        in_specs=[pl.BlockSpec((4, HEAD_DIM), lambda b, i: (0, 0)),
                  pl.BlockSpec((DIFF_V_DIM, 1), lambda b, i: (0, 0)),
                  pl.BlockSpec((None, ATT_TQ, DIFF_QK_WIDTH), lambda b, i: (b, i, 0)),
                  pl.BlockSpec((None, SEQ, DIFF_QK_WIDTH), lambda b, i: (b, 0, 1)),
                  pl.BlockSpec((None, SEQ, DIFF_V_WIDTH), lambda b, i: (b, 0, v_tile))],
        out_specs=[pl.BlockSpec((None, ATT_TQ, DIFF_V_WIDTH), lambda b, i: (b, i, 0))],
        out_shape=[jax.ShapeDtypeStruct((BATCH, SEQ, DIFF_V_WIDTH), BF16)],
        scratch_shapes=[pltpu.VMEM((SEQ // ATT_BK, DIFF_V_WIDTH, ATT_BK), BF16),
                        pltpu.VMEM((DIFF_MAPS, DIFF_V_DIM, ATT_TQ), F32),
                        pltpu.VMEM((DIFF_MAPS, 1, ATT_TQ), F32),
                        pltpu.VMEM((DIFF_MAPS, 1, ATT_TQ), F32),
                        pltpu.VMEM((DIFF_MAPS, ATT_BK, ATT_TQ), F32),
                        pltpu.VMEM((DIFF_QK_WIDTH, ATT_TQ), BF16)],
        args=(lam_rows, subln_col, qk3, qk3, proj3),
        casts=casts)


def _out_proj_kernel(*refs, n_in, rider):
    a_refs = refs[:n_in]
    w_refs = refs[n_in:2 * n_in]
    g_ref, h_ref, o_ref = refs[2 * n_in:]
    rider()
    m = jnp.dot(a_refs[0][...], w_refs[0][...], preferred_element_type=F32)
    for a_ref, w_ref in zip(a_refs[1:], w_refs[1:]):
        m = m + jnp.dot(a_ref[...], w_ref[...], preferred_element_type=F32)
    o_ref[...] = m
    _add_normalized_rows(h_ref, o_ref, g_ref, o_ref)


def _out_proj(acts, w, g, h, name, casts=()):
    n_in = len(acts)
    k = w.shape[0] // n_in
    in_specs = ([pl.BlockSpec((OUT_TM, k), lambda i: (i, 0)) for _ in acts]
                + [pl.BlockSpec((k, D_MODEL), functools.partial(lambda i, r: (r, 0), r=r))
                   for r in range(n_in)]
                + [pl.BlockSpec((1, D_MODEL), lambda i: (0, 0)),
                   pl.BlockSpec((OUT_TM, D_MODEL), lambda i: (i, 0))])
    return _call(
        functools.partial(_out_proj_kernel, n_in=n_in),
        name=name,
        grid=(TOKENS // OUT_TM,),
        in_specs=in_specs,
        out_specs=[pl.BlockSpec((OUT_TM, D_MODEL), lambda i: (i, 0))],
        out_shape=[jax.ShapeDtypeStruct((TOKENS, D_MODEL), F32)],
        args=(*acts, *([w] * n_in), g, h),
        casts=casts)


def _ffn_kernel(h_ref, g1_ref, w1_ref, w2_ref, g2_ref, o_ref, hn_ref, *, rider):
    f = pl.program_id(1)

    def partial_product():
        rider()
        a = jnp.maximum(jnp.dot(hn_ref[...], w1_ref[...], preferred_element_type=F32), 0.0)
        return jnp.dot((a * a).astype(BF16), w2_ref[...], preferred_element_type=F32)

    @pl.when(f == 0)
    def _():
        _normalize_rows(h_ref, g1_ref, hn_ref)
        o_ref[...] = partial_product()

    last = pl.num_programs(1) - 1

    @pl.when(jnp.logical_and(f != 0, f != last))
    def _():
        o_ref[...] += partial_product()

    @pl.when(f == last)
    def _():
        o_ref[...] += partial_product()
        _add_normalized_rows(h_ref, o_ref, g2_ref, o_ref)


def _ffn(h, g1, w1, w2, g2, casts):
    assert D_FF // FFN_TF >= 2, "the first and the last hidden step are distinct code paths"
    return _call(
        _ffn_kernel,
        name="ffn",
        grid=(TOKENS // FFN_TM, D_FF // FFN_TF),
        in_specs=[pl.BlockSpec((FFN_TM, D_MODEL), lambda i, f: (i, 0)),
                  pl.BlockSpec((1, D_MODEL), lambda i, f: (0, 0)),
                  pl.BlockSpec((D_MODEL, FFN_TF), lambda i, f: (0, f)),
                  pl.BlockSpec((FFN_TF, D_MODEL), lambda i, f: (f, 0)),
                  pl.BlockSpec((1, D_MODEL), lambda i, f: (0, 0))],
        out_specs=[pl.BlockSpec((FFN_TM, D_MODEL), lambda i, f: (i, 0))],
        out_shape=[jax.ShapeDtypeStruct((TOKENS, D_MODEL), F32)],
        scratch_shapes=[pltpu.VMEM((FFN_TM, D_MODEL), BF16)],
        args=(h, g1, w1, w2, g2),
        casts=casts)


def _ple_kernel(h_ref, p_ref, wg_ref, wp_ref, g_ref, o_ref, *, rider):
    rider()
    z = jnp.dot(h_ref[...].astype(BF16), wg_ref[...], preferred_element_type=F32)
    e = jnp.dot(p_ref[...].astype(BF16), wp_ref[...], preferred_element_type=F32)
    o_ref[...] = e / (1.0 + jnp.exp2(z * (-LOG2E)))
    _add_normalized_rows(h_ref, o_ref, g_ref, o_ref)


def _ple(h, p_stack, layer, wg, wp, g, casts=()):
    return _call(
        _ple_kernel,
        name="ple",
        grid=(TOKENS // PLE_TM,),
        in_specs=[pl.BlockSpec((PLE_TM, D_MODEL), lambda i: (i, 0)),
                  pl.BlockSpec((None, PLE_TM, PLE_DIM), lambda i: (layer, i, 0)),
                  pl.BlockSpec((D_MODEL, D_MODEL), lambda i: (0, 0)),
                  pl.BlockSpec((PLE_DIM, D_MODEL), lambda i: (0, 0)),
                  pl.BlockSpec((1, D_MODEL), lambda i: (0, 0))],
        out_specs=[pl.BlockSpec((PLE_TM, D_MODEL), lambda i: (i, 0))],
        out_shape=[jax.ShapeDtypeStruct((TOKENS, D_MODEL), F32)],
        args=(h, p_stack, wg, wp, g),
        casts=casts)


def _sg_kernel(u_ref, v_ref, lng_ref, lnb_ref, ws_ref, bst_ref, y_ref, vn_ref, *, rider):
    rider()
    ln_g = lng_ref[...]
    ln_b = lnb_ref[...]
    for r0 in range(0, SG_TM, NORM_ROWS):
        rows = slice(r0, r0 + NORM_ROWS)
        v = v_ref[rows, :].astype(F32)
        vc = v - jnp.mean(v, axis=-1, keepdims=True)
        inv = lax.rsqrt(jnp.mean(vc * vc, axis=-1, keepdims=True) + EPS)
        vn_ref[rows, :] = (vc * inv * ln_g + ln_b).astype(BF16)

    n_blk = SG_TM // SG_BLOCK
    t_idx = lax.broadcasted_iota(jnp.int32, (SG_BLOCK, SG_BLOCK), 0)
    s_idx = lax.broadcasted_iota(jnp.int32, (SG_BLOCK, SG_BLOCK), 1)
    visible = _chunk_of(s_idx) <= _chunk_of(t_idx)
    bst = bst_ref[...]
    for g in range(SG_GROUPS):
        cols = slice(g * SG_GROUP_DIM, (g + 1) * SG_GROUP_DIM)
        w = jnp.where(visible, ws_ref[g], 0.0).astype(BF16)
        rhs = jnp.concatenate(
            [vn_ref[n * SG_BLOCK:(n + 1) * SG_BLOCK, cols] for n in range(n_blk)], axis=1)
        mixed = jnp.dot(w, rhs, preferred_element_type=F32) + bst[:, g:g + 1]
        for n in range(n_blk):
            rows = slice(n * SG_BLOCK, (n + 1) * SG_BLOCK)
            u = u_ref[rows, cols].astype(F32)
            y_ref[rows, cols] = (u * mixed[:, n * SG_BLOCK:(n + 1) * SG_BLOCK]).astype(y_ref.dtype)


def _spatial_gate(uv, ln_g, ln_b, w_s_stack, layer, b_s_t, casts):
    return _call(
        _sg_kernel,
        name="spatial_gate",
        grid=(TOKENS // SG_TM,),
        in_specs=[pl.BlockSpec((SG_TM, SG_WIDTH), lambda i: (i, 0)),
                  pl.BlockSpec((SG_TM, SG_WIDTH), lambda i: (i, 1)),
                  pl.BlockSpec((1, SG_WIDTH), lambda i: (0, 0)),
                  pl.BlockSpec((1, SG_WIDTH), lambda i: (0, 0)),
                  pl.BlockSpec((None, SG_GROUPS, SG_BLOCK, SG_BLOCK), lambda i: (layer, 0, 0, 0)),
                  pl.BlockSpec((SG_BLOCK, SG_GROUPS), lambda i: (0, 0))],
        out_specs=[pl.BlockSpec((SG_TM, SG_WIDTH), lambda i: (i, 0))],
        out_shape=[jax.ShapeDtypeStruct((TOKENS, SG_WIDTH), BF16)],
        scratch_shapes=[pltpu.VMEM((SG_TM, SG_WIDTH), BF16)],
        args=(uv, uv, ln_g, ln_b, w_s_stack, b_s_t),
        casts=casts)


def _row(v):
    return v.reshape(1, -1)


def kernel(x, p, positions, ev_norm_pre, ev_w_in, ev_lam_q1, ev_lam_k1, ev_lam_q2, ev_lam_k2,
           ev_subln, ev_w_out, ev_norm_post, od_norm_pre, od_w_in, od_ln_g, od_ln_b, od_w_s,
           od_b_s, od_w_out, od_norm_post, ffn_norm_pre, ffn_w1, ffn_w2, ffn_norm_post,
           ple_w_proj, ple_w_gate, ple_norm):
    assert DEPTH == 2, "the cast schedule below is written for one even and one odd layer"
    h = x.reshape(TOKENS, D_MODEL)
    p_stack = p.reshape(DEPTH, TOKENS, PLE_DIM)
    (cos_t, sin_t), (w_in0,) = _rope_tables(positions, casts=[_Cast(ev_w_in, 0)])
    sb_q_scale = jnp.concatenate([jnp.full((1, SB_WIDTH), Q_SCALE, F32),
                                  jnp.ones((1, (len(PLAIN_W_TILES) - 1) * PROJ_TN), F32)], axis=1)

    lambda_init = 0.8 - 0.6 * math.exp(-0.3 * 0)
    (proj, xn), (w_out0,) = _plain_proj(h, _row(ev_norm_pre[0]), w_in0, sb_q_scale,
                                        casts=[_Cast(ev_w_out, 0)])
    (qk,), (ffn_w2_0,) = _rope_proj(xn, w_in0, cos_t, sin_t, casts=[_Cast(ffn_w2, 0)])
    proj3 = proj.reshape(BATCH, SEQ, len(PLAIN_W_TILES) * PROJ_TN)
    qk3 = qk.reshape(BATCH, SEQ, 2 * DIFF_QK_WIDTH)
    (sb_o,), (ffn_w1_0,) = _sb_attention(proj3, casts=[_Cast(ffn_w1, 0)])
    lam_rows = jnp.stack([ev_lam_q1[0], ev_lam_k1[0], ev_lam_q2[0], ev_lam_k2[0]])
    (df_o,), (gate_0, pproj_0) = _diff_attention(
        qk3, proj3, lam_rows, ev_subln[0].reshape(DIFF_V_DIM, 1), lambda_init,
        casts=[_Cast(ple_w_gate, 0), _Cast(ple_w_proj, 0)])
    (h,), _ = _out_proj([sb_o.reshape(TOKENS, SB_WIDTH), df_o.reshape(TOKENS, DIFF_V_WIDTH)],
                        w_out0, _row(ev_norm_post[0]), h, "even_out_proj")
    (h,), _ = _ffn(h, _row(ffn_norm_pre[0]), ffn_w1_0, ffn_w2_0, _row(ffn_norm_post[0]), casts=[])
    (h,), (od_in, od_out) = _ple(h, p_stack, 0, gate_0, pproj_0, _row(ple_norm[0]),
                                 casts=[_Cast(od_w_in, 0), _Cast(od_w_out, 0)])

    (uv,), (ffn_w1_1,) = _odd_proj(h, _row(od_norm_pre[0]), od_in, casts=[_Cast(ffn_w1, 1)])
    (y,), (gate_1, pproj_1) = _spatial_gate(
        uv, _row(od_ln_g[0]), _row(od_ln_b[0]), od_w_s, 0, od_b_s[0].T,
        casts=[_Cast(ple_w_gate, 1), _Cast(ple_w_proj, 1)])
    (h,), (ffn_w2_1,) = _out_proj([y], od_out, _row(od_norm_post[0]), h, "odd_out_proj",
                                  casts=[_Cast(ffn_w2, 1)])
    (h,), _ = _ffn(h, _row(ffn_norm_pre[1]), ffn_w1_1, ffn_w2_1, _row(ffn_norm_post[1]), casts=[])
    (h,), _ = _ple(h, p_stack, 1, gate_1, pproj_1, _row(ple_norm[1]))
    return h.reshape(BATCH, SEQ, D_MODEL)
```
